```python
import jax, jax.numpy as jnp
from jax import lax
import numpy as np

D_MODEL = 1024
BATCH = 4
SEQ = 4096
DEPTH = 1

CHUNK = 64
Q_BLOCK = 128
HEAD_DIM = 64
N_SB_HEADS = 8
N_FOX_HEADS = 8
SB_WIDTH = N_SB_HEADS * HEAD_DIM
FOX_WIDTH = N_FOX_HEADS * HEAD_DIM
N_BRANCHES = 2
D_FF = 2816
CONV_WIDTH = 3
EPS = 1e-6
OFF_SB = 0
OFF_FOX = OFF_SB + 3 * SB_WIDTH
OFF_FORGET = OFF_FOX + 3 * FOX_WIDTH
OFF_GATE = OFF_FORGET + N_FOX_HEADS
IN_COLS = OFF_GATE + N_BRANCHES * D_MODEL

kernel_name = "hybrid_stickbreak_fox_convffn"


def rms_norm(x, g):
    xf = x.astype(jnp.float32)
    y = xf * lax.rsqrt(jnp.mean(xf * xf, axis=-1, keepdims=True) + EPS)
    return y.astype(x.dtype) * g


def split_heads(t, n_heads):
    b, s, _ = t.shape
    return t.reshape(b, s, n_heads, HEAD_DIM).transpose(0, 2, 1, 3)


def merge_heads(t):
    b, h, s, d = t.shape
    return t.transpose(0, 2, 1, 3).reshape(b, s, h * d)


def stick_breaking_attention(q, k, v):
    seq = q.shape[2]
    scale = HEAD_DIM ** -0.5
    outs = []
    for start in range(0, seq, Q_BLOCK):
        end = start + Q_BLOCK
        qb = q[:, :, start:end]
        kp, vp = k[:, :, :end], v[:, :, :end]
        z = jnp.einsum("bhqd,bhkd->bhqk", qb, kp).astype(jnp.float32) * scale
        q_pos = start + jnp.arange(Q_BLOCK)[:, None]
        k_pos = jnp.arange(end)[None, :]
        strict = k_pos < q_pos
        log_fail = jnp.where(strict, jax.nn.log_sigmoid(-z), 0.0)
        after = lax.cumsum(log_fail, axis=3, reverse=True) - log_fail
        w = jnp.where(strict, jnp.exp(jax.nn.log_sigmoid(z) + after), 0.0)
        outs.append(jnp.einsum("bhqk,bhkd->bhqd", w.astype(v.dtype), vp))
    return jnp.concatenate(outs, axis=2)


def forgetting_attention(q, k, v, log_f):
    seq = q.shape[2]
    scale = HEAD_DIM ** -0.5
    c = lax.cumsum(log_f, axis=2)
    outs = []
    for start in range(0, seq, Q_BLOCK):
        end = start + Q_BLOCK
        qb = q[:, :, start:end]
        kp, vp = k[:, :, :end], v[:, :, :end]
        z = jnp.einsum("bhqd,bhkd->bhqk", qb, kp).astype(jnp.float32) * scale
        z = z + c[:, :, start:end, None] - c[:, :, None, :end]
        q_pos = start + jnp.arange(Q_BLOCK)[:, None]
        k_pos = jnp.arange(end)[None, :]
        z = jnp.where(k_pos <= q_pos, z, -jnp.inf)
        p = jax.nn.softmax(z, axis=-1)
        outs.append(jnp.einsum("bhqk,bhkd->bhqd", p.astype(v.dtype), vp))
    return jnp.concatenate(outs, axis=2)


def causal_depthwise_conv(u, w, b):
    c = u.shape[-1]
    y = lax.conv_general_dilated(
        u, w.astype(u.dtype)[:, None, :],
        window_strides=(1,), padding=[(CONV_WIDTH - 1, 0)],
        dimension_numbers=("NWC", "WIO", "NWC"), feature_group_count=c)
    return y + b


def setup_inputs(seed: int = 0) -> dict:
    key = jax.random.key(seed)
    ks = jax.random.split(key, 16)
    nrm = lambda k, shape, fan_in: jax.random.normal(k, shape, jnp.float32) * fan_in ** -0.5
    gain = lambda k, shape: 1.0 + 0.1 * jax.random.normal(k, shape, jnp.float32)
    return {
        "x": jax.random.normal(ks[0], (BATCH, SEQ, D_MODEL), jnp.float32),
        "g_mix": gain(ks[1], (DEPTH, D_MODEL)),
        "w_in": nrm(ks[2], (DEPTH, D_MODEL, IN_COLS), D_MODEL),
        "b_forget": jax.random.uniform(ks[3], (DEPTH, N_FOX_HEADS), jnp.float32, 1.0, 5.0),
        "b_gate": 0.01 * jax.random.normal(ks[4], (DEPTH, N_BRANCHES * D_MODEL), jnp.float32),
        "g_q": gain(ks[5], (DEPTH, HEAD_DIM)),
        "g_k": gain(ks[6], (DEPTH, HEAD_DIM)),
        "w_o_sb": nrm(ks[7], (DEPTH, SB_WIDTH, D_MODEL), SB_WIDTH),
        "w_o_fox": nrm(ks[8], (DEPTH, FOX_WIDTH, D_MODEL), FOX_WIDTH),
        "w_out": nrm(ks[9], (DEPTH, D_MODEL, D_MODEL), D_MODEL),
        "g_ffn": gain(ks[10], (DEPTH, D_MODEL)),
        "w_up": nrm(ks[11], (DEPTH, D_MODEL, 2 * D_FF), D_MODEL),
        "conv_w": nrm(ks[12], (DEPTH, CONV_WIDTH, 2 * D_FF), CONV_WIDTH),
        "conv_b": 0.01 * jax.random.normal(ks[13], (DEPTH, 2 * D_FF), jnp.float32),
        "w_down": nrm(ks[14], (DEPTH, D_FF, D_MODEL), D_FF),
    }


def reference(x, g_mix, w_in, b_forget, b_gate, g_q, g_k, w_o_sb, w_o_fox, w_out,
              g_ffn, w_up, conv_w, conv_b, w_down):
    b, s, _ = x.shape
    for layer in range(DEPTH):
        h = rms_norm(x, g_mix[layer])
        proj = h @ w_in[layer]
        q_sb = split_heads(proj[..., OFF_SB:OFF_SB + SB_WIDTH], N_SB_HEADS)
        k_sb = split_heads(proj[..., OFF_SB + SB_WIDTH:OFF_SB + 2 * SB_WIDTH], N_SB_HEADS)
        v_sb = split_heads(proj[..., OFF_SB + 2 * SB_WIDTH:OFF_FOX], N_SB_HEADS)
        q_fx = split_heads(proj[..., OFF_FOX:OFF_FOX + FOX_WIDTH], N_FOX_HEADS)
        k_fx = split_heads(proj[..., OFF_FOX + FOX_WIDTH:OFF_FOX + 2 * FOX_WIDTH], N_FOX_HEADS)
        v_fx = split_heads(proj[..., OFF_FOX + 2 * FOX_WIDTH:OFF_FORGET], N_FOX_HEADS)
        q_fx = rms_norm(q_fx, g_q[layer])
        k_fx = rms_norm(k_fx, g_k[layer])
        f_logit = (proj[..., OFF_FORGET:OFF_GATE] + b_forget[layer]).astype(jnp.float32)
        log_f = jax.nn.log_sigmoid(f_logit).transpose(0, 2, 1)
        gates = jax.nn.sigmoid(proj[..., OFF_GATE:] + b_gate[layer]).reshape(b, s, N_BRANCHES, D_MODEL)

        y_sb = merge_heads(stick_breaking_attention(q_sb, k_sb, v_sb)) @ w_o_sb[layer]
        y_fx = merge_heads(forgetting_attention(q_fx, k_fx, v_fx, log_f)) @ w_o_fox[layer]
        mixed = gates[:, :, 0] * y_sb + gates[:, :, 1] * y_fx
        x = x + mixed @ w_out[layer]

        h = rms_norm(x, g_ffn[layer])
        u = causal_depthwise_conv(h @ w_up[layer], conv_w[layer], conv_b[layer])
        gate, val = u[..., :D_FF], u[..., D_FF:]
        x = x + (jax.nn.silu(gate) * val) @ w_down[layer]
    return x
```

```python
import functools

import jax
import jax.numpy as jnp
from jax import lax
from jax.experimental import pallas as pl
from jax.experimental.pallas import tpu as pltpu

D_MODEL = 1024
HEAD_DIM = 64
N_SB_HEADS = 8
N_FOX_HEADS = 8
SB_WIDTH = N_SB_HEADS * HEAD_DIM
FOX_WIDTH = N_FOX_HEADS * HEAD_DIM
D_FF = 2816
CONV_WIDTH = 3
EPS = 1e-6
OFF_FOX = 3 * SB_WIDTH
OFF_FORGET = OFF_FOX + 3 * FOX_WIDTH
OFF_GATE = OFF_FORGET + N_FOX_HEADS

LANES = 128
HEADS_PER_BLOCK = LANES // HEAD_DIM
BF16_ROWS = 16
NEG_BIG = -1e30

F32 = jnp.float32
BF16 = jnp.bfloat16

ROW_TILE = 512
FFN_ROW_TILE = 1024
FFN_COL_TILE = 256
Q_TILE = 512
SB_KEY_CHUNK = 256
SB_SUM_BLOCK = 128
FOX_KEY_CHUNK = 512
VMEM_LIMIT = 56 * 1024 * 1024


def _dot(a, b):
    return jnp.dot(a, b, preferred_element_type=F32)


def _dot_nt(a, b):
    return lax.dot_general(a, b, (((1,), (1,)), ((), ())), preferred_element_type=F32)


def _rms_norm_rows(x, g):
    return x * lax.rsqrt(jnp.mean(x * x, axis=-1, keepdims=True) + EPS) * g


def _cumsum_rows(x):
    n = x.shape[0]
    row = lax.broadcasted_iota(jnp.int32, x.shape, 0)
    step = 1
    while step < n:
        x = x + jnp.where(row >= step, pltpu.roll(x, step, axis=0), 0.0)
        step *= 2
    return x


def _in_proj_kernel(x_ref, g_ref, wsb_ref, wfx_ref, wf_ref, bf_ref, gq_ref, gk_ref, seg_ref,
                    qsb_ref, ksb_ref, vsb_ref, qfx_ref, kfx_ref, vfx_ref, ccol_ref, crow_ref,
                    carry_ref, *, tiles_per_seq):
    scale = HEAD_DIM ** -0.5
    h = _rms_norm_rows(x_ref[...], g_ref[...]).astype(BF16)

    psb = _dot(h, wsb_ref[...])
    qsb_ref[...] = (psb[:, :SB_WIDTH] * scale).astype(BF16)
    ksb_ref[...] = psb[:, SB_WIDTH:2 * SB_WIDTH].astype(BF16)
    vsb_ref[...] = psb[:, 2 * SB_WIDTH:].astype(BF16)

    pfx = _dot(h, wfx_ref[...])
    seg = seg_ref[...]

    def head_norm(t, g):
        ms = _dot((t * t).astype(BF16), seg)
        return t * lax.rsqrt(ms + EPS) * g

    qfx_ref[...] = (head_norm(pfx[:, :FOX_WIDTH], gq_ref[...]) * scale).astype(BF16)
    kfx_ref[...] = head_norm(pfx[:, FOX_WIDTH:2 * FOX_WIDTH], gk_ref[...]).astype(BF16)
    vfx_ref[...] = pfx[:, 2 * FOX_WIDTH:].astype(BF16)

    @pl.when(pl.program_id(0) % tiles_per_seq == 0)
    def _():
        carry_ref[...] = jnp.zeros_like(carry_ref)

    log_f = jax.nn.log_sigmoid(_dot(h, wf_ref[...]) + bf_ref[...])
    c = _cumsum_rows(log_f) + carry_ref[0:1, :]
    ccol_ref[...] = c
    carry_ref[0:1, :] = c[-1:, :]
    crow_ref[...] = c.T[:N_FOX_HEADS, :]


def _in_proj(x2d, g_mix, w_sb, w_fx, w_f, b_f, g_q, g_k, seg, *, seq):
    m = x2d.shape[0]
    tm = ROW_TILE
    const = lambda i: (0, 0)
    row = lambda i: (i, 0)
    full = lambda a: pl.BlockSpec(a.shape, const)
    out_w = lambda: pl.BlockSpec((tm, SB_WIDTH), row)
    return pl.pallas_call(
        functools.partial(_in_proj_kernel, tiles_per_seq=seq // tm),
        grid=(m // tm,),
        in_specs=[pl.BlockSpec((tm, D_MODEL), row), full(g_mix), full(w_sb), full(w_fx), full(w_f),
                  full(b_f), full(g_q), full(g_k), full(seg)],
        out_specs=[out_w(), out_w(), out_w(), out_w(), out_w(), out_w(),
                   pl.BlockSpec((tm, LANES), row),
                   pl.BlockSpec((N_FOX_HEADS, tm), lambda i: (0, i))],
        out_shape=[jax.ShapeDtypeStruct((m, SB_WIDTH), BF16)] * 6
        + [jax.ShapeDtypeStruct((m, LANES), F32), jax.ShapeDtypeStruct((N_FOX_HEADS, m), F32)],
        scratch_shapes=[pltpu.VMEM((8, LANES), F32)],
        compiler_params=pltpu.CompilerParams(dimension_semantics=("arbitrary",),
                                             vmem_limit_bytes=VMEM_LIMIT),
        name="in_proj",
    )(x2d, g_mix, w_sb, w_fx, w_f, b_f, g_q, g_k, seg)


def _head_lane_mask(head_in_block):
    lane = lax.broadcasted_iota(jnp.int32, (1, LANES), 1)
    return (lane >= head_in_block * HEAD_DIM) & (lane < (head_in_block + 1) * HEAD_DIM)


def _suffix_sum_weights():
    n = 2 * SB_SUM_BLOCK
    r = lax.broadcasted_iota(jnp.int32, (n, n), 0) % SB_SUM_BLOCK
    c = lax.broadcasted_iota(jnp.int32, (n, n), 1)
    return jnp.where((c >= SB_SUM_BLOCK) | (r >= c), 1.0, 0.0).astype(BF16)


def _sb_chunk(qm, k_chunk, vm_chunk, carry, tsum, strict):
    z = _dot_nt(qm, k_chunk)
    lf = -(jnp.maximum(z, 0.0) + jnp.log1p(jnp.exp(-jnp.abs(z))))
    if strict is not None:
        lf = jnp.where(strict, lf, 0.0)
    ws = [None, None]
    for half in (1, 0):
        cols = slice(half * SB_SUM_BLOCK, (half + 1) * SB_SUM_BLOCK)
        x = lf[:, cols]
        x_hi = x.astype(BF16)
        x_lo = (x - x_hi.astype(F32)).astype(BF16)
        sums = _dot(jnp.concatenate([x_hi, x_lo], axis=1), tsum)
        w = jnp.exp(z[:, cols] + sums[:, :SB_SUM_BLOCK] + carry)
        if strict is not None:
            w = jnp.where(strict[:, cols], w, 0.0)
        ws[half] = w.astype(BF16)
        carry = carry + sums[:, SB_SUM_BLOCK:]
    return _dot(jnp.concatenate(ws, axis=1), vm_chunk), carry


def _sb_kernel(q_ref, k_ref, v_ref, o_ref, acc_ref, carry_ref):
    tq, kc = Q_TILE, SB_KEY_CHUNK
    q0 = pl.program_id(2) * tq
    tsum = _suffix_sum_weights()
    q = q_ref[...]
    acc_ref[...] = jnp.zeros_like(acc_ref)
    for hh in range(HEADS_PER_BLOCK):
        lanes = _head_lane_mask(hh)
        qm = jnp.where(lanes, q, jnp.zeros_like(q))
        carry_ref[...] = jnp.zeros_like(carry_ref)

        for jd in reversed(range(tq // kc)):
            r0 = jd * kc
            k0 = pl.multiple_of(q0 + r0, kc)
            qpos = r0 + lax.broadcasted_iota(jnp.int32, (tq - r0, kc), 0)
            kpos = r0 + lax.broadcasted_iota(jnp.int32, (tq - r0, kc), 1)
            vm = jnp.where(lanes, v_ref[pl.ds(k0, kc), :], jnp.zeros((kc, LANES), BF16))
            pv, carry = _sb_chunk(qm[r0:, :], k_ref[pl.ds(k0, kc), :], vm, carry_ref[r0:, :], tsum,
                                  kpos < qpos)
            acc_ref[r0:, :] += pv
            carry_ref[r0:, :] = carry

        def body(it, _):
            k0 = pl.multiple_of(q0 - (it + 1) * kc, kc)
            vm = jnp.where(lanes, v_ref[pl.ds(k0, kc), :], jnp.zeros((kc, LANES), BF16))
            pv, carry = _sb_chunk(qm, k_ref[pl.ds(k0, kc), :], vm, carry_ref[...], tsum, None)
            acc_ref[...] += pv
            carry_ref[...] = carry
            return 0

        lax.fori_loop(0, q0 // kc, body, 0)
    o_ref[...] = acc_ref[...].astype(BF16)


def _sb_attention(q, k, v, *, batch, seq):
    m = q.shape[0]
    nq = seq // Q_TILE
    n_blocks = SB_WIDTH // LANES
    q_spec = pl.BlockSpec((Q_TILE, LANES), lambda b, hp, qi: (b * nq + qi, hp))
    kv_spec = pl.BlockSpec((seq, LANES), lambda b, hp, qi: (b, hp))
    return pl.pallas_call(
        _sb_kernel,
        grid=(batch, n_blocks, nq),
        in_specs=[q_spec, kv_spec, kv_spec],
        out_specs=q_spec,
        out_shape=jax.ShapeDtypeStruct((m, SB_WIDTH), BF16),
        scratch_shapes=[pltpu.VMEM((Q_TILE, LANES), F32), pltpu.VMEM((Q_TILE, LANES), F32)],
        compiler_params=pltpu.CompilerParams(dimension_semantics=("arbitrary",) * 3,
                                             vmem_limit_bytes=VMEM_LIMIT),
        name="sb_attn",
    )(q, k, v)


def _fox_kernel(q_ref, k_ref, v_ref, ccol_ref, crow_ref, o_ref, m_ref, l_ref, acc_ref):
    tq, kc = Q_TILE, FOX_KEY_CHUNK
    hp = pl.program_id(1)
    q0 = pl.program_id(2) * tq
    q = q_ref[...]
    ccol = ccol_ref[...]
    lane = lax.broadcasted_iota(jnp.int32, (1, LANES), 1)
    sub = lax.broadcasted_iota(jnp.int32, (N_FOX_HEADS, 1), 0)
    out = jnp.zeros((tq, LANES), F32)
    for hh in range(HEADS_PER_BLOCK):
        head = hp * HEADS_PER_BLOCK + hh
        lanes = _head_lane_mask(hh)
        qm = jnp.where(lanes, q, jnp.zeros_like(q))
        cq = jnp.sum(jnp.where(lane == head, ccol, 0.0), axis=1, keepdims=True)
        m_ref[...] = jnp.full_like(m_ref, NEG_BIG)
        l_ref[...] = jnp.zeros_like(l_ref)
        acc_ref[...] = jnp.zeros_like(acc_ref)

        def chunk(r0, k0, causal):
            ck = jnp.sum(jnp.where(sub == head, crow_ref[:, pl.ds(k0, kc)], 0.0), axis=0, keepdims=True)
            z = _dot_nt(qm[r0:, :], k_ref[pl.ds(k0, kc), :]) + cq[r0:, :] - ck
            if causal:
                qpos = lax.broadcasted_iota(jnp.int32, (tq - r0, kc), 0)
                kpos = lax.broadcasted_iota(jnp.int32, (tq - r0, kc), 1)
                z = jnp.where(kpos <= qpos, z, NEG_BIG)
            m_prev = m_ref[r0:, :]
            m_new = jnp.maximum(m_prev, jnp.max(z, axis=1, keepdims=True))
            alpha = jnp.exp(m_prev - m_new)
            p = jnp.exp(z - m_new)
            l_ref[r0:, :] = alpha * l_ref[r0:, :] + jnp.sum(p, axis=1, keepdims=True)
            vm = jnp.where(lanes, v_ref[pl.ds(k0, kc), :], jnp.zeros((kc, LANES), BF16))
            acc_ref[r0:, :] = alpha * acc_ref[r0:, :] + _dot(p.astype(BF16), vm)
            m_ref[r0:, :] = m_new

        for jd in reversed(range(tq // kc)):
            chunk(jd * kc, pl.multiple_of(q0 + jd * kc, kc), True)

        def body(it, _):
            chunk(0, pl.multiple_of(it * kc, kc), False)
            return 0

        lax.fori_loop(0, q0 // kc, body, 0)
        out = out + acc_ref[...] / l_ref[...]
    o_ref[...] = out.astype(BF16)


def _fox_attention(q, k, v, c_col, c_row, *, batch, seq):
    m = q.shape[0]
    nq = seq // Q_TILE
    n_blocks = FOX_WIDTH // LANES
    q_spec = pl.BlockSpec((Q_TILE, LANES), lambda b, hp, qi: (b * nq + qi, hp))
    kv_spec = pl.BlockSpec((seq, LANES), lambda b, hp, qi: (b, hp))
    return pl.pallas_call(
        _fox_kernel,
        grid=(batch, n_blocks, nq),
        in_specs=[q_spec, kv_spec, kv_spec,
                  pl.BlockSpec((Q_TILE, LANES), lambda b, hp, qi: (b * nq + qi, 0)),
                  pl.BlockSpec((N_FOX_HEADS, seq), lambda b, hp, qi: (0, b))],
        out_specs=q_spec,
        out_shape=jax.ShapeDtypeStruct((m, FOX_WIDTH), BF16),
        scratch_shapes=[pltpu.VMEM((Q_TILE, 1), F32), pltpu.VMEM((Q_TILE, 1), F32),
                        pltpu.VMEM((Q_TILE, LANES), F32)],
        compiler_params=pltpu.CompilerParams(dimension_semantics=("arbitrary",) * 3,
                                             vmem_limit_bytes=VMEM_LIMIT),
        name="fox_attn",
    )(q, k, v, c_col, c_row)


def _out_proj_kernel(x_ref, g_ref, wg_ref, bg_ref, osb_ref, ofx_ref, wosb_ref, wofx_ref, wout_ref, o_ref):
    x = x_ref[...]
    h = _rms_norm_rows(x, g_ref[...]).astype(BF16)
    gates = jax.nn.sigmoid(_dot(h, wg_ref[...]) + bg_ref[...])
    y_sb = _dot(osb_ref[...], wosb_ref[...])
    y_fx = _dot(ofx_ref[...], wofx_ref[...])
    mixed = gates[:, :D_MODEL] * y_sb + gates[:, D_MODEL:] * y_fx
    o_ref[...] = x + _dot(mixed.astype(BF16), wout_ref[...])


def _out_proj(x2d, g_mix, w_gate, b_gate, o_sb, o_fx, w_o_sb, w_o_fox, w_out):
    m = x2d.shape[0]
    tm = ROW_TILE
    const = lambda i: (0, 0)
    row = lambda i: (i, 0)
    full = lambda a: pl.BlockSpec(a.shape, const)
    return pl.pallas_call(
        _out_proj_kernel,
        grid=(m // tm,),
        in_specs=[pl.BlockSpec((tm, D_MODEL), row), full(g_mix), full(w_gate), full(b_gate),
                  pl.BlockSpec((tm, SB_WIDTH), row), pl.BlockSpec((tm, FOX_WIDTH), row),
                  full(w_o_sb), full(w_o_fox), full(w_out)],
        out_specs=pl.BlockSpec((tm, D_MODEL), row),
        out_shape=jax.ShapeDtypeStruct((m, D_MODEL), F32),
        compiler_params=pltpu.CompilerParams(dimension_semantics=("arbitrary",),
                                             vmem_limit_bytes=VMEM_LIMIT),
        name="out_proj",
    )(x2d, g_mix, w_gate, b_gate, o_sb, o_fx, w_o_sb, w_o_fox, w_out)


def _conv_ffn_kernel(x_ref, g_ref, wug_ref, wuv_ref, cwg_ref, cwv_ref, cbg_ref, cbv_ref, wd_ref, o_ref,
                     hext_ref, prev_ref, acc_ref, *, tiles_per_seq):
    tm, halo = FFN_ROW_TILE, BF16_ROWS
    i, f = pl.program_id(0), pl.program_id(1)

    @pl.when(f == 0)
    def _():
        h = _rms_norm_rows(x_ref[...], g_ref[...]).astype(BF16)
        first = i % tiles_per_seq == 0
        hext_ref[:halo, :] = jnp.where(first, jnp.zeros_like(prev_ref), prev_ref[...])
        hext_ref[halo:, :] = h
        prev_ref[...] = h[tm - halo:, :]
        acc_ref[...] = jnp.zeros_like(acc_ref)

    hext = hext_ref[...]

    def conv(w_ref, cw_ref, cb_ref):
        u = _dot(hext, w_ref[...])
        cw = cw_ref[...]
        y = cw[2:3, :] * u[halo:, :] + cb_ref[...]
        y = y + cw[1:2, :] * pltpu.roll(u, 1, axis=0)[halo:, :]
        return y + cw[0:1, :] * pltpu.roll(u, 2, axis=0)[halo:, :]

    gate = conv(wug_ref, cwg_ref, cbg_ref)
    val = conv(wuv_ref, cwv_ref, cbv_ref)
    act = (gate * jax.nn.sigmoid(gate) * val).astype(BF16)
    acc_ref[...] += _dot(act, wd_ref[...])

    @pl.when(f == pl.num_programs(1) - 1)
    def _():
        o_ref[...] = x_ref[...] + acc_ref[...]


def _conv_ffn(x2d, g_ffn, w_up, conv_w, conv_b, w_down, *, seq):
    m = x2d.shape[0]
    tm, tf = FFN_ROW_TILE, FFN_COL_TILE
    nf = D_FF // tf
    row = lambda i, f: (i, 0)
    gcol = lambda i, f: (0, f)
    vcol = lambda i, f: (0, nf + f)
    return pl.pallas_call(
        functools.partial(_conv_ffn_kernel, tiles_per_seq=seq // tm),
        grid=(m // tm, nf),
        in_specs=[pl.BlockSpec((tm, D_MODEL), row), pl.BlockSpec(g_ffn.shape, lambda i, f: (0, 0)),
                  pl.BlockSpec((D_MODEL, tf), gcol), pl.BlockSpec((D_MODEL, tf), vcol),
                  pl.BlockSpec((CONV_WIDTH, tf), gcol), pl.BlockSpec((CONV_WIDTH, tf), vcol),
                  pl.BlockSpec((1, tf), gcol), pl.BlockSpec((1, tf), vcol),
                  pl.BlockSpec((tf, D_MODEL), lambda i, f: (f, 0))],
        out_specs=pl.BlockSpec((tm, D_MODEL), row),
        out_shape=jax.ShapeDtypeStruct((m, D_MODEL), F32),
        scratch_shapes=[pltpu.VMEM((tm + BF16_ROWS, D_MODEL), BF16), pltpu.VMEM((BF16_ROWS, D_MODEL), BF16),
                        pltpu.VMEM((tm, D_MODEL), F32)],
        compiler_params=pltpu.CompilerParams(dimension_semantics=("arbitrary", "arbitrary"),
                                             vmem_limit_bytes=VMEM_LIMIT),
        name="conv_ffn",
    )(x2d, g_ffn, w_up, w_up, conv_w, conv_w, conv_b, conv_b, w_down)


def kernel(x, g_mix, w_in, b_forget, b_gate, g_q, g_k, w_o_sb, w_o_fox, w_out,
           g_ffn, w_up, conv_w, conv_b, w_down):
    batch, seq, _ = x.shape
    depth = g_mix.shape[0]
    assert seq % Q_TILE == 0 and seq % FFN_ROW_TILE == 0 and seq % ROW_TILE == 0
    x2d = x.reshape(batch * seq, D_MODEL)
    head_of_lane = jnp.arange(FOX_WIDTH) // HEAD_DIM
    seg = jnp.where(head_of_lane[:, None] == head_of_lane[None, :], 1.0 / HEAD_DIM, 0.0).astype(BF16)
    row2d = lambda a: a.reshape(1, -1)
    for layer in range(depth):
        w = w_in[layer].astype(BF16)
        w_f = jnp.pad(w[:, OFF_FORGET:OFF_GATE], ((0, 0), (0, LANES - N_FOX_HEADS)))
        b_f = jnp.pad(b_forget[layer], (0, LANES - N_FOX_HEADS)).reshape(1, LANES)
        q_sb, k_sb, v_sb, q_fx, k_fx, v_fx, c_col, c_row = _in_proj(
            x2d, row2d(g_mix[layer]), w[:, :OFF_FOX], w[:, OFF_FOX:OFF_FORGET], w_f, b_f,
            row2d(jnp.tile(g_q[layer], N_FOX_HEADS)), row2d(jnp.tile(g_k[layer], N_FOX_HEADS)), seg, seq=seq)
        o_sb = _sb_attention(q_sb, k_sb, v_sb, batch=batch, seq=seq)
        o_fx = _fox_attention(q_fx, k_fx, v_fx, c_col, c_row, batch=batch, seq=seq)
        x2d = _out_proj(x2d, row2d(g_mix[layer]), w[:, OFF_GATE:], row2d(b_gate[layer]), o_sb, o_fx,
                        w_o_sb[layer].astype(BF16), w_o_fox[layer].astype(BF16), w_out[layer].astype(BF16))
        x2d = _conv_ffn(x2d, row2d(g_ffn[layer]), w_up[layer].astype(BF16), conv_w[layer],
                        row2d(conv_b[layer]), w_down[layer].astype(BF16), seq=seq)
    return x2d.reshape(batch, seq, D_MODEL)
```

```python
import functools

import jax
import jax.numpy as jnp
from jax import lax
from jax.experimental import pallas as pl
from jax.experimental.pallas import tpu as pltpu

D_MODEL = 1024
HEAD_DIM = 64
N_SB_HEADS = 8
N_FOX_HEADS = 8
SB_WIDTH = N_SB_HEADS * HEAD_DIM
FOX_WIDTH = N_FOX_HEADS * HEAD_DIM
D_FF = 2816
CONV_WIDTH = 3
EPS = 1e-6
OFF_FOX = 3 * SB_WIDTH
OFF_FORGET = OFF_FOX + 3 * FOX_WIDTH
OFF_GATE = OFF_FORGET + N_FOX_HEADS

LANES = 128
HEADS_PER_BLOCK = LANES // HEAD_DIM
BF16_ROWS = 16
NEG_BIG = -1e30
LOG2E = 1.4426950408889634
SB_UNDERFLOW_LOG2 = -160.0

F32 = jnp.float32
BF16 = jnp.bfloat16

ROW_TILE = 512
FFN_ROW_TILE = 1024
FFN_COL_TILE = 256
Q_TILE = 512
SB_KEY_CHUNK = 256
FOX_KEY_CHUNK = 512
VMEM_LIMIT = 56 * 1024 * 1024


def _dot(a, b):
    return jnp.dot(a, b, preferred_element_type=F32)


def _dot_nt(a, b):
    return lax.dot_general(a, b, (((1,), (1,)), ((), ())), preferred_element_type=F32)


def _rms_norm_rows(x, g):
    return x * lax.rsqrt(jnp.mean(x * x, axis=-1, keepdims=True) + EPS) * g


def _cumsum_rows(x):
    n = x.shape[0]
    row = lax.broadcasted_iota(jnp.int32, x.shape, 0)
    step = 1
    while step < n:
        x = x + jnp.where(row >= step, pltpu.roll(x, step, axis=0), 0.0)
        step *= 2
    return x


def _in_proj_kernel(x_ref, g_ref, wsb_ref, wfx_ref, wf_ref, bf_ref, gq_ref, gk_ref, seg_ref,
                    qsb_ref, ksb_ref, vsb_ref, qfx_ref, kfx_ref, vfx_ref, crow_ref,
                    carry_ref, *, tiles_per_seq):
    scale = HEAD_DIM ** -0.5 * LOG2E
    h = _rms_norm_rows(x_ref[...], g_ref[...]).astype(BF16)

    psb = _dot(h, wsb_ref[...])
    qsb_ref[...] = (psb[:, :SB_WIDTH] * scale).astype(BF16)
    ksb_ref[...] = psb[:, SB_WIDTH:2 * SB_WIDTH].astype(BF16)
    vsb_ref[...] = psb[:, 2 * SB_WIDTH:].astype(BF16)

    pfx = _dot(h, wfx_ref[...])
    seg = seg_ref[...]

    def head_norm(t, g):
        ms = _dot((t * t).astype(BF16), seg)
        return t * lax.rsqrt(ms + EPS) * g

    qfx_ref[...] = (head_norm(pfx[:, :FOX_WIDTH], gq_ref[...]) * scale).astype(BF16)
    kfx_ref[...] = head_norm(pfx[:, FOX_WIDTH:2 * FOX_WIDTH], gk_ref[...]).astype(BF16)
    vfx_ref[...] = pfx[:, 2 * FOX_WIDTH:].astype(BF16)

    @pl.when(pl.program_id(0) % tiles_per_seq == 0)
    def _():
        carry_ref[...] = jnp.zeros_like(carry_ref)

    log_f = jax.nn.log_sigmoid(_dot(h, wf_ref[...]) + bf_ref[...])
    c = _cumsum_rows(log_f) + carry_ref[0:1, :]
    carry_ref[0:1, :] = c[-1:, :]
    crow_ref[...] = (c * LOG2E).T[:N_FOX_HEADS, :]


def _in_proj(x2d, g_mix, w_sb, w_fx, w_f, b_f, g_q, g_k, seg, *, seq):
    m = x2d.shape[0]
    tm = ROW_TILE
    const = lambda i: (0, 0)
    row = lambda i: (i, 0)
    full = lambda a: pl.BlockSpec(a.shape, const)
    out_w = lambda: pl.BlockSpec((tm, SB_WIDTH), row)
    return pl.pallas_call(
        functools.partial(_in_proj_kernel, tiles_per_seq=seq // tm),
        grid=(m // tm,),
        in_specs=[pl.BlockSpec((tm, D_MODEL), row), full(g_mix), full(w_sb), full(w_fx), full(w_f),
                  full(b_f), full(g_q), full(g_k), full(seg)],
        out_specs=[out_w(), out_w(), out_w(), out_w(), out_w(), out_w(),
                   pl.BlockSpec((N_FOX_HEADS, tm), lambda i: (0, i))],
        out_shape=[jax.ShapeDtypeStruct((m, SB_WIDTH), BF16)] * 6
        + [jax.ShapeDtypeStruct((N_FOX_HEADS, m), F32)],
        scratch_shapes=[pltpu.VMEM((8, LANES), F32)],
        compiler_params=pltpu.CompilerParams(dimension_semantics=("arbitrary",),
                                             vmem_limit_bytes=VMEM_LIMIT),
        name="in_proj",
    )(x2d, g_mix, w_sb, w_fx, w_f, b_f, g_q, g_k, seg)


def _head_lane_masks():
    lane = lax.broadcasted_iota(jnp.int32, (1, LANES), 1)
    return [(lane >= hh * HEAD_DIM) & (lane < (hh + 1) * HEAD_DIM) for hh in range(HEADS_PER_BLOCK)]


def _neg_abs(x):
    bits = lax.bitcast_convert_type(x, jnp.uint32) | jnp.uint32(0x80000000)
    return lax.bitcast_convert_type(bits, F32)


def _sb_weights(q, km, carry, later_sum, strict):
    z = _dot_nt(q, km)
    sp = jnp.maximum(z, 0.0) + jnp.log2(1.0 + jnp.exp2(_neg_abs(z)))
    if strict is not None:
        sp = jnp.where(strict, sp, 0.0)
    later = _dot(sp.astype(BF16), later_sum)
    w = jnp.exp2((z - sp) + later + jnp.concatenate([carry] * (z.shape[1] // LANES), axis=1))
    if strict is not None:
        w = jnp.where(strict, w, 0.0)
    total = later[:, :1] - sp[:, :1]
    return w.astype(BF16), carry + jnp.broadcast_to(total, carry.shape)


def _sb_kernel(q_ref, k_ref, v_ref, o_ref, acc_ref, carry_ref):
    tq, kc = Q_TILE, SB_KEY_CHUNK
    q0 = pl.program_id(2) * tq
    lanes = _head_lane_masks()
    later_sum = jnp.where(lax.broadcasted_iota(jnp.int32, (kc, kc), 0) > lax.broadcasted_iota(jnp.int32, (kc, kc), 1),
                          -1.0, 0.0).astype(BF16)
    acc_ref[...] = jnp.zeros_like(acc_ref)
    carry_ref[...] = jnp.zeros_like(carry_ref)

    def step(r0, k0, strict):
        k_chunk = k_ref[pl.ds(k0, kc), :]
        v_chunk = v_ref[pl.ds(k0, kc), :]
        zero = jnp.zeros_like(k_chunk)
        ws, vms = [], []
        for hh in range(HEADS_PER_BLOCK):
            w, carry = _sb_weights(q_ref[r0:, :], jnp.where(lanes[hh], k_chunk, zero), carry_ref[hh, r0:, :],
                                   later_sum, strict)
            carry_ref[hh, r0:, :] = carry
            ws.append(w)
            vms.append(jnp.where(lanes[hh], v_chunk, zero))
        acc_ref[r0:, :] += _dot(jnp.concatenate(ws, axis=1), jnp.concatenate(vms, axis=0))

    for jd in reversed(range(tq // kc)):
        r0 = jd * kc
        qpos = lax.broadcasted_iota(jnp.int32, (tq - r0, kc), 0)
        kpos = lax.broadcasted_iota(jnp.int32, (tq - r0, kc), 1)
        step(r0, pl.multiple_of(q0 + r0, kc), kpos < qpos)

    def body(state):
        it, _ = state
        step(0, pl.multiple_of(q0 - (it + 1) * kc, kc), None)
        return it + 1, (jnp.max(carry_ref[...]) > SB_UNDERFLOW_LOG2).astype(jnp.int32)

    lax.while_loop(lambda s: (s[0] < q0 // kc) & (s[1] > 0), body, (jnp.int32(0), jnp.int32(1)))
    o_ref[...] = acc_ref[...].astype(BF16)


def _sb_attention(q, k, v, *, batch, seq):
    m = q.shape[0]
    nq = seq // Q_TILE
    n_blocks = SB_WIDTH // LANES
    q_spec = pl.BlockSpec((Q_TILE, LANES), lambda b, hp, qi: (b * nq + qi, hp))
    kv_spec = pl.BlockSpec((seq, LANES), lambda b, hp, qi: (b, hp))
    return pl.pallas_call(
        _sb_kernel,
        grid=(batch, n_blocks, nq),
        in_specs=[q_spec, kv_spec, kv_spec],
        out_specs=q_spec,
        out_shape=jax.ShapeDtypeStruct((m, SB_WIDTH), BF16),
        scratch_shapes=[pltpu.VMEM((Q_TILE, LANES), F32),
                        pltpu.VMEM((HEADS_PER_BLOCK, Q_TILE, LANES), F32)],
        compiler_params=pltpu.CompilerParams(dimension_semantics=("arbitrary",) * 3,
                                             vmem_limit_bytes=VMEM_LIMIT),
        name="sb_attn",
    )(q, k, v)


def _fox_kernel(q_ref, k_ref, v_ref, crow_ref, o_ref, z_ref, m_ref, acc_ref):
    tq, kc = Q_TILE, FOX_KEY_CHUNK
    hp = pl.program_id(1)
    q0 = pl.program_id(2) * tq
    lanes = _head_lane_masks()
    sub = lax.broadcasted_iota(jnp.int32, (N_FOX_HEADS, 1), 0)
    m_ref[...] = jnp.full_like(m_ref, NEG_BIG)

    def logits(k0, causal):
        k_chunk = k_ref[pl.ds(k0, kc), :]
        c_rows = crow_ref[:, pl.ds(k0, kc)]
        zero = jnp.zeros_like(k_chunk)
        for hh in range(HEADS_PER_BLOCK):
            ck = jnp.sum(jnp.where(sub == hp * HEADS_PER_BLOCK + hh, c_rows, 0.0), axis=0, keepdims=True)
            z = _dot_nt(q_ref[...], jnp.where(lanes[hh], k_chunk, zero)) - ck
            if causal:
                qpos = lax.broadcasted_iota(jnp.int32, (tq, kc), 0)
                kpos = lax.broadcasted_iota(jnp.int32, (tq, kc), 1)
                z = jnp.where(kpos <= qpos, z, NEG_BIG)
            z_ref[hh, :, pl.ds(k0, kc)] = z
            mx = m_ref[hh]
            for c in range(kc // LANES):
                mx = jnp.maximum(mx, z[:, c * LANES:(c + 1) * LANES])
            m_ref[hh] = mx

    def logits_body(j, _):
        logits(pl.multiple_of(j * kc, kc), False)
        return 0

    lax.fori_loop(0, q0 // kc, logits_body, 0)
    logits(pl.multiple_of(q0, kc), True)

    for hh in range(HEADS_PER_BLOCK):
        m_ref[hh] = jnp.broadcast_to(jnp.max(m_ref[hh], axis=1, keepdims=True), (tq, LANES))
    acc_ref[...] = jnp.zeros_like(acc_ref)

    def weigh_body(j, _):
        k0 = pl.multiple_of(j * kc, kc)
        v_chunk = v_ref[pl.ds(k0, kc), :]
        one = jnp.ones_like(v_chunk)
        for hh in range(HEADS_PER_BLOCK):
            row_max = m_ref[hh]
            p = [jnp.exp2(z_ref[hh, :, pl.ds(k0 + c * LANES, LANES)] - row_max) for c in range(kc // LANES)]
            acc_ref[hh] += _dot(jnp.concatenate(p, axis=1).astype(BF16), jnp.where(lanes[hh], v_chunk, one))
        return 0

    lax.fori_loop(0, q0 // kc + 1, weigh_body, 0)

    out = jnp.zeros((tq, LANES), F32)
    for hh in range(HEADS_PER_BLOCK):
        acc = acc_ref[hh]
        out = out + jnp.where(lanes[hh], acc / pltpu.roll(acc, HEAD_DIM, axis=1), 0.0)
    o_ref[...] = out.astype(BF16)


def _fox_attention(q, k, v, c_row, *, batch, seq):
    assert Q_TILE == FOX_KEY_CHUNK
    m = q.shape[0]
    nq = seq // Q_TILE
    n_blocks = FOX_WIDTH // LANES
    q_spec = pl.BlockSpec((Q_TILE, LANES), lambda b, hp, qi: (b * nq + qi, hp))
    kv_spec = pl.BlockSpec((seq, LANES), lambda b, hp, qi: (b, hp))
    return pl.pallas_call(
        _fox_kernel,
        grid=(batch, n_blocks, nq),
        in_specs=[q_spec, kv_spec, kv_spec,
                  pl.BlockSpec((N_FOX_HEADS, seq), lambda b, hp, qi: (0, b))],
        out_specs=q_spec,
        out_shape=jax.ShapeDtypeStruct((m, FOX_WIDTH), BF16),
        scratch_shapes=[pltpu.VMEM((HEADS_PER_BLOCK, Q_TILE, seq), F32),
                        pltpu.VMEM((HEADS_PER_BLOCK, Q_TILE, LANES), F32),
                        pltpu.VMEM((HEADS_PER_BLOCK, Q_TILE, LANES), F32)],
        compiler_params=pltpu.CompilerParams(dimension_semantics=("arbitrary",) * 3,
                                             vmem_limit_bytes=VMEM_LIMIT),
        name="fox_attn",
    )(q, k, v, c_row)


def _out_proj_kernel(x_ref, g_ref, wg_ref, bg_ref, osb_ref, ofx_ref, wosb_ref, wofx_ref, wout_ref, o_ref):
    x = x_ref[...]
    h = _rms_norm_rows(x, g_ref[...]).astype(BF16)
    gates = jax.nn.sigmoid(_dot(h, wg_ref[...]) + bg_ref[...])
    y_sb = _dot(osb_ref[...], wosb_ref[...])
    y_fx = _dot(ofx_ref[...], wofx_ref[...])
    mixed = gates[:, :D_MODEL] * y_sb + gates[:, D_MODEL:] * y_fx
    o_ref[...] = x + _dot(mixed.astype(BF16), wout_ref[...])


def _out_proj(x2d, g_mix, w_gate, b_gate, o_sb, o_fx, w_o_sb, w_o_fox, w_out):
    m = x2d.shape[0]
    tm = ROW_TILE
    const = lambda i: (0, 0)
    row = lambda i: (i, 0)
    full = lambda a: pl.BlockSpec(a.shape, const)
    return pl.pallas_call(
        _out_proj_kernel,
        grid=(m // tm,),
        in_specs=[pl.BlockSpec((tm, D_MODEL), row), full(g_mix), full(w_gate), full(b_gate),
                  pl.BlockSpec((tm, SB_WIDTH), row), pl.BlockSpec((tm, FOX_WIDTH), row),
                  full(w_o_sb), full(w_o_fox), full(w_out)],
        out_specs=pl.BlockSpec((tm, D_MODEL), row),
        out_shape=jax.ShapeDtypeStruct((m, D_MODEL), F32),
        compiler_params=pltpu.CompilerParams(dimension_semantics=("arbitrary",),
                                             vmem_limit_bytes=VMEM_LIMIT),
        name="out_proj",
    )(x2d, g_mix, w_gate, b_gate, o_sb, o_fx, w_o_sb, w_o_fox, w_out)


def _conv_ffn_kernel(x_ref, g_ref, wug_ref, wuv_ref, cwg_ref, cwv_ref, cbg_ref, cbv_ref, wd_ref, o_ref,
                     hext_ref, prev_ref, acc_ref, *, tiles_per_seq):
    tm, halo = FFN_ROW_TILE, BF16_ROWS
    i, f = pl.program_id(0), pl.program_id(1)

    @pl.when(f == 0)
    def _():
        h = _rms_norm_rows(x_ref[...], g_ref[...]).astype(BF16)
        first = i % tiles_per_seq == 0
        hext_ref[:halo, :] = jnp.where(first, jnp.zeros_like(prev_ref), prev_ref[...])
        hext_ref[halo:, :] = h
        prev_ref[...] = h[tm - halo:, :]
        acc_ref[...] = jnp.zeros_like(acc_ref)

    hext = hext_ref[...]

    def conv(w_ref, cw_ref, cb_ref):
        u = _dot(hext, w_ref[...])
        cw = cw_ref[...]
        y = cw[2:3, :] * u[halo:, :] + cb_ref[...]
        y = y + cw[1:2, :] * pltpu.roll(u, 1, axis=0)[halo:, :]
        return y + cw[0:1, :] * pltpu.roll(u, 2, axis=0)[halo:, :]

    gate = conv(wug_ref, cwg_ref, cbg_ref)
    val = conv(wuv_ref, cwv_ref, cbv_ref)
    act = (gate * jax.nn.sigmoid(gate) * val).astype(BF16)
    acc_ref[...] += _dot(act, wd_ref[...])

    @pl.when(f == pl.num_programs(1) - 1)
    def _():
        o_ref[...] = x_ref[...] + acc_ref[...]


def _conv_ffn(x2d, g_ffn, w_up, conv_w, conv_b, w_down, *, seq):
    m = x2d.shape[0]
    tm, tf = FFN_ROW_TILE, FFN_COL_TILE
    nf = D_FF // tf
    row = lambda i, f: (i, 0)
    gcol = lambda i, f: (0, f)
    vcol = lambda i, f: (0, nf + f)
    return pl.pallas_call(
        functools.partial(_conv_ffn_kernel, tiles_per_seq=seq // tm),
        grid=(m // tm, nf),
        in_specs=[pl.BlockSpec((tm, D_MODEL), row), pl.BlockSpec(g_ffn.shape, lambda i, f: (0, 0)),
                  pl.BlockSpec((D_MODEL, tf), gcol), pl.BlockSpec((D_MODEL, tf), vcol),
                  pl.BlockSpec((CONV_WIDTH, tf), gcol), pl.BlockSpec((CONV_WIDTH, tf), vcol),
                  pl.BlockSpec((1, tf), gcol), pl.BlockSpec((1, tf), vcol),
                  pl.BlockSpec((tf, D_MODEL), lambda i, f: (f, 0))],
        out_specs=pl.BlockSpec((tm, D_MODEL), row),
        out_shape=jax.ShapeDtypeStruct((m, D_MODEL), F32),
        scratch_shapes=[pltpu.VMEM((tm + BF16_ROWS, D_MODEL), BF16), pltpu.VMEM((BF16_ROWS, D_MODEL), BF16),
                        pltpu.VMEM((tm, D_MODEL), F32)],
        compiler_params=pltpu.CompilerParams(dimension_semantics=("arbitrary", "arbitrary"),
                                             vmem_limit_bytes=VMEM_LIMIT),
        name="conv_ffn",
    )(x2d, g_ffn, w_up, w_up, conv_w, conv_w, conv_b, conv_b, w_down)


def kernel(x, g_mix, w_in, b_forget, b_gate, g_q, g_k, w_o_sb, w_o_fox, w_out,
           g_ffn, w_up, conv_w, conv_b, w_down):
    batch, seq, _ = x.shape
    depth = g_mix.shape[0]
    assert seq % Q_TILE == 0 and seq % FFN_ROW_TILE == 0 and seq % ROW_TILE == 0
    x2d = x.reshape(batch * seq, D_MODEL)
    head_of_lane = jnp.arange(FOX_WIDTH) // HEAD_DIM
    seg = jnp.where(head_of_lane[:, None] == head_of_lane[None, :], 1.0 / HEAD_DIM, 0.0).astype(BF16)
    row2d = lambda a: a.reshape(1, -1)
    for layer in range(depth):
        w = w_in[layer].astype(BF16)
        w_f = jnp.pad(w[:, OFF_FORGET:OFF_GATE], ((0, 0), (0, LANES - N_FOX_HEADS)))
        b_f = jnp.pad(b_forget[layer], (0, LANES - N_FOX_HEADS)).reshape(1, LANES)
        q_sb, k_sb, v_sb, q_fx, k_fx, v_fx, c_row = _in_proj(
            x2d, row2d(g_mix[layer]), w[:, :OFF_FOX], w[:, OFF_FOX:OFF_FORGET], w_f, b_f,
            row2d(jnp.tile(g_q[layer], N_FOX_HEADS)), row2d(jnp.tile(g_k[layer], N_FOX_HEADS)), seg, seq=seq)
        o_sb = _sb_attention(q_sb, k_sb, v_sb, batch=batch, seq=seq)
        o_fx = _fox_attention(q_fx, k_fx, v_fx, c_row, batch=batch, seq=seq)
        x2d = _out_proj(x2d, row2d(g_mix[layer]), w[:, OFF_GATE:], row2d(b_gate[layer]), o_sb, o_fx,
                        w_o_sb[layer].astype(BF16), w_o_fox[layer].astype(BF16), w_out[layer].astype(BF16))
        x2d = _conv_ffn(x2d, row2d(g_ffn[layer]), w_up[layer].astype(BF16), conv_w[layer],
                        row2d(conv_b[layer]), w_down[layer].astype(BF16), seq=seq)
    return x2d.reshape(batch, seq, D_MODEL)
```

```python
import functools

import jax
import jax.numpy as jnp
from jax import lax
from jax.experimental import pallas as pl
from jax.experimental.pallas import tpu as pltpu

D_MODEL = 1024
HEAD_DIM = 64
N_SB_HEADS = 8
N_FOX_HEADS = 8
SB_WIDTH = N_SB_HEADS * HEAD_DIM
FOX_WIDTH = N_FOX_HEADS * HEAD_DIM
D_FF = 2816
CONV_WIDTH = 3
EPS = 1e-6
OFF_FOX = 3 * SB_WIDTH
OFF_FORGET = OFF_FOX + 3 * FOX_WIDTH
OFF_GATE = OFF_FORGET + N_FOX_HEADS

LANES = 128
HEADS_PER_BLOCK = LANES // HEAD_DIM
BF16_ROWS = 16
NEG_BIG = -1e30
LOG2E = 1.4426950408889634
SB_UNDERFLOW_LOG2 = -160.0

F32 = jnp.float32
BF16 = jnp.bfloat16

ROW_TILE = 512
FFN_ROW_TILE = 1024
FFN_COL_TILE = 256
FFN_ACT_ROWS = 128
Q_TILE = 512
SB_KEY_CHUNK = 256
FOX_KEY_CHUNK = 512
FOX_WIDE_CHUNK = 2 * FOX_KEY_CHUNK
VMEM_LIMIT = 56 * 1024 * 1024


def _dot(a, b):
    return jnp.dot(a, b, preferred_element_type=F32)


def _dot_nt(a, b):
    return lax.dot_general(a, b, (((1,), (1,)), ((), ())), preferred_element_type=F32)


def _rms_norm_rows(x, g):
    return x * lax.rsqrt(jnp.mean(x * x, axis=-1, keepdims=True) + EPS) * g


def _cumsum_rows(x):
    n = x.shape[0]
    row = lax.broadcasted_iota(jnp.int32, x.shape, 0)
    step = 1
    while step < n:
        x = x + jnp.where(row >= step, pltpu.roll(x, step, axis=0), 0.0)
        step *= 2
    return x


def _in_proj_kernel(x_ref, g_ref, wsb_ref, wfx_ref, wf_ref, bf_ref, gq_ref, gk_ref, seg_ref,
                    qsb_ref, ksb_ref, vsb_ref, qfx_ref, kfx_ref, vfx_ref, crow_ref,
                    carry_ref, *, tiles_per_seq):
    scale = HEAD_DIM ** -0.5 * LOG2E
    h = _rms_norm_rows(x_ref[...], g_ref[...]).astype(BF16)

    psb = _dot(h, wsb_ref[...])
    qsb_ref[...] = (psb[:, :SB_WIDTH] * scale).astype(BF16)
    ksb_ref[...] = psb[:, SB_WIDTH:2 * SB_WIDTH].astype(BF16)
    vsb_ref[...] = psb[:, 2 * SB_WIDTH:].astype(BF16)

    pfx = _dot(h, wfx_ref[...])
    seg = seg_ref[...]

    def head_norm(t, g):
        ms = _dot((t * t).astype(BF16), seg)
        return t * lax.rsqrt(ms + EPS) * g

    qfx_ref[...] = (head_norm(pfx[:, :FOX_WIDTH], gq_ref[...]) * scale).astype(BF16)
    kfx_ref[...] = head_norm(pfx[:, FOX_WIDTH:2 * FOX_WIDTH], gk_ref[...]).astype(BF16)
    vfx_ref[...] = pfx[:, 2 * FOX_WIDTH:].astype(BF16)

    @pl.when(pl.program_id(0) % tiles_per_seq == 0)
    def _():
        carry_ref[...] = jnp.zeros_like(carry_ref)

    log_f = jax.nn.log_sigmoid(_dot(h, wf_ref[...]) + bf_ref[...])
    c = _cumsum_rows(log_f) + carry_ref[0:1, :]
    carry_ref[0:1, :] = c[-1:, :]
    crow_ref[...] = (c * LOG2E).T[:N_FOX_HEADS, :]


def _in_proj(x2d, g_mix, w_sb, w_fx, w_f, b_f, g_q, g_k, seg, *, seq):
    m = x2d.shape[0]
    tm = ROW_TILE
    const = lambda i: (0, 0)
    row = lambda i: (i, 0)
    full = lambda a: pl.BlockSpec(a.shape, const)
    out_w = lambda: pl.BlockSpec((tm, SB_WIDTH), row)
    return pl.pallas_call(
        functools.partial(_in_proj_kernel, tiles_per_seq=seq // tm),
        grid=(m // tm,),
        in_specs=[pl.BlockSpec((tm, D_MODEL), row), full(g_mix), full(w_sb), full(w_fx), full(w_f),
                  full(b_f), full(g_q), full(g_k), full(seg)],
        out_specs=[out_w(), out_w(), out_w(), out_w(), out_w(), out_w(),
                   pl.BlockSpec((N_FOX_HEADS, tm), lambda i: (0, i))],
        out_shape=[jax.ShapeDtypeStruct((m, SB_WIDTH), BF16)] * 6
        + [jax.ShapeDtypeStruct((N_FOX_HEADS, m), F32)],
        scratch_shapes=[pltpu.VMEM((8, LANES), F32)],
        compiler_params=pltpu.CompilerParams(dimension_semantics=("arbitrary",),
                                             vmem_limit_bytes=VMEM_LIMIT),
        name="in_proj",
    )(x2d, g_mix, w_sb, w_fx, w_f, b_f, g_q, g_k, seg)


def _head_lane_masks():
    lane = lax.broadcasted_iota(jnp.int32, (1, LANES), 1)
    return [(lane >= hh * HEAD_DIM) & (lane < (hh + 1) * HEAD_DIM) for hh in range(HEADS_PER_BLOCK)]


def _neg_abs(x):
    bits = lax.bitcast_convert_type(x, jnp.uint32) | jnp.uint32(0x80000000)
    return lax.bitcast_convert_type(bits, F32)


def _sb_weights(q, km, carry, later_sum, strict):
    z = _dot_nt(q, km)
    sp = jnp.maximum(z, 0.0) + jnp.log2(1.0 + jnp.exp2(_neg_abs(z)))
    if strict is not None:
        sp = jnp.where(strict, sp, 0.0)
    later = _dot(sp.astype(BF16), later_sum)
    w = jnp.exp2((z - sp) + later + jnp.concatenate([carry] * (z.shape[1] // LANES), axis=1))
    if strict is not None:
        w = jnp.where(strict, w, 0.0)
    total = later[:, :1] - sp[:, :1]
    return w.astype(BF16), carry + jnp.broadcast_to(total, carry.shape)


def _sb_kernel(q_ref, k_ref, v_ref, o_ref, acc_ref, carry_ref):
    tq, kc = Q_TILE, SB_KEY_CHUNK
    q0 = pl.program_id(2) * tq
    lanes = _head_lane_masks()
    later_sum = jnp.where(lax.broadcasted_iota(jnp.int32, (kc, kc), 0) > lax.broadcasted_iota(jnp.int32, (kc, kc), 1),
                          -1.0, 0.0).astype(BF16)
    acc_ref[...] = jnp.zeros_like(acc_ref)
    carry_ref[...] = jnp.zeros_like(carry_ref)

    def step(r0, k0, strict):
        k_chunk = k_ref[pl.ds(k0, kc), :]
        v_chunk = v_ref[pl.ds(k0, kc), :]
        zero = jnp.zeros_like(k_chunk)
        ws, vms = [], []
        for hh in range(HEADS_PER_BLOCK):
            w, carry = _sb_weights(q_ref[r0:, :], jnp.where(lanes[hh], k_chunk, zero), carry_ref[hh, r0:, :],
                                   later_sum, strict)
            carry_ref[hh, r0:, :] = carry
            ws.append(w)
            vms.append(jnp.where(lanes[hh], v_chunk, zero))
        acc_ref[r0:, :] += _dot(jnp.concatenate(ws, axis=1), jnp.concatenate(vms, axis=0))

    for jd in reversed(range(tq // kc)):
        r0 = jd * kc
        qpos = lax.broadcasted_iota(jnp.int32, (tq - r0, kc), 0)
        kpos = lax.broadcasted_iota(jnp.int32, (tq - r0, kc), 1)
        step(r0, pl.multiple_of(q0 + r0, kc), kpos < qpos)

    def body(state):
        it, _ = state
        step(0, pl.multiple_of(q0 - (it + 1) * kc, kc), None)
        return it + 1, (jnp.max(carry_ref[...]) > SB_UNDERFLOW_LOG2).astype(jnp.int32)

    lax.while_loop(lambda s: (s[0] < q0 // kc) & (s[1] > 0), body, (jnp.int32(0), jnp.int32(1)))
    o_ref[...] = acc_ref[...].astype(BF16)


def _sb_attention(q, k, v, *, batch, seq):
    m = q.shape[0]
    nq = seq // Q_TILE
    n_blocks = SB_WIDTH // LANES
    q_spec = pl.BlockSpec((Q_TILE, LANES), lambda b, hp, qi: (b * nq + qi, hp))
    kv_spec = pl.BlockSpec((seq, LANES), lambda b, hp, qi: (b, hp))
    return pl.pallas_call(
        _sb_kernel,
        grid=(batch, n_blocks, nq),
        in_specs=[q_spec, kv_spec, kv_spec],
        out_specs=q_spec,
        out_shape=jax.ShapeDtypeStruct((m, SB_WIDTH), BF16),
        scratch_shapes=[pltpu.VMEM((Q_TILE, LANES), F32),
                        pltpu.VMEM((HEADS_PER_BLOCK, Q_TILE, LANES), F32)],
        compiler_params=pltpu.CompilerParams(dimension_semantics=("arbitrary",) * 3,
                                             vmem_limit_bytes=VMEM_LIMIT),
        name="sb_attn",
    )(q, k, v)


def _fox_kernel(q_ref, k_ref, v_ref, crow_ref, o_ref, z_ref, m_ref, acc_ref):
    tq, kc, wide = Q_TILE, FOX_KEY_CHUNK, FOX_WIDE_CHUNK
    hp = pl.program_id(1)
    q0 = pl.program_id(2) * tq
    lanes = _head_lane_masks()
    sub = lax.broadcasted_iota(jnp.int32, (N_FOX_HEADS, 1), 0)
    m_ref[...] = jnp.full_like(m_ref, NEG_BIG)

    def logits(k0, width, causal):
        k_chunk = k_ref[pl.ds(k0, width), :]
        c_rows = crow_ref[:, pl.ds(k0, width)]
        zero = jnp.zeros_like(k_chunk)
        for hh in range(HEADS_PER_BLOCK):
            ck = jnp.sum(jnp.where(sub == hp * HEADS_PER_BLOCK + hh, c_rows, 0.0), axis=0, keepdims=True)
            z = _dot_nt(q_ref[...], jnp.where(lanes[hh], k_chunk, zero)) - ck
            if causal:
                qpos = lax.broadcasted_iota(jnp.int32, (tq, width), 0)
                kpos = lax.broadcasted_iota(jnp.int32, (tq, width), 1)
                z = jnp.where(kpos <= qpos, z, NEG_BIG)
            z_ref[hh, :, pl.ds(k0, width)] = z
            mx = m_ref[hh]
            for c in range(width // LANES):
                mx = jnp.maximum(mx, z[:, c * LANES:(c + 1) * LANES])
            m_ref[hh] = mx

    def weigh(k0, width):
        v_chunk = v_ref[pl.ds(k0, width), :]
        one = jnp.ones_like(v_chunk)
        for hh in range(HEADS_PER_BLOCK):
            row_max = m_ref[hh]
            p = [jnp.exp2(z_ref[hh, :, pl.ds(k0 + c * LANES, LANES)] - row_max) for c in range(width // LANES)]
            acc_ref[hh] += _dot(jnp.concatenate(p, axis=1).astype(BF16), jnp.where(lanes[hh], v_chunk, one))

    def walk(n_keys, fn):
        n_wide = n_keys // wide

        def body(j, _):
            fn(pl.multiple_of(j * wide, wide), wide)
            return 0

        lax.fori_loop(0, n_wide, body, 0)

        @pl.when(n_keys % wide != 0)
        def _():
            fn(pl.multiple_of(n_wide * wide, kc), kc)

    walk(q0, lambda k0, width: logits(k0, width, False))
    logits(pl.multiple_of(q0, kc), kc, True)

    for hh in range(HEADS_PER_BLOCK):
        m_ref[hh] = jnp.broadcast_to(jnp.max(m_ref[hh], axis=1, keepdims=True), (tq, LANES))
    acc_ref[...] = jnp.zeros_like(acc_ref)
    walk(q0 + tq, weigh)

    out = jnp.zeros((tq, LANES), F32)
    for hh in range(HEADS_PER_BLOCK):
        acc = acc_ref[hh]
        out = out + jnp.where(lanes[hh], acc / pltpu.roll(acc, HEAD_DIM, axis=1), 0.0)
    o_ref[...] = out.astype(BF16)


def _fox_attention(q, k, v, c_row, *, batch, seq):
    assert Q_TILE == FOX_KEY_CHUNK
    m = q.shape[0]
    nq = seq // Q_TILE
    n_blocks = FOX_WIDTH // LANES
    q_spec = pl.BlockSpec((Q_TILE, LANES), lambda b, hp, qi: (b * nq + qi, hp))
    kv_spec = pl.BlockSpec((seq, LANES), lambda b, hp, qi: (b, hp))
    return pl.pallas_call(
        _fox_kernel,
        grid=(batch, n_blocks, nq),
        in_specs=[q_spec, kv_spec, kv_spec,
                  pl.BlockSpec((N_FOX_HEADS, seq), lambda b, hp, qi: (0, b))],
        out_specs=q_spec,
        out_shape=jax.ShapeDtypeStruct((m, FOX_WIDTH), BF16),
        scratch_shapes=[pltpu.VMEM((HEADS_PER_BLOCK, Q_TILE, seq), F32),
                        pltpu.VMEM((HEADS_PER_BLOCK, Q_TILE, LANES), F32),
                        pltpu.VMEM((HEADS_PER_BLOCK, Q_TILE, LANES), F32)],
        compiler_params=pltpu.CompilerParams(dimension_semantics=("arbitrary",) * 3,
                                             vmem_limit_bytes=VMEM_LIMIT),
        name="fox_attn",
    )(q, k, v, c_row)


def _out_proj_kernel(x_ref, g_ref, wg_ref, bg_ref, osb_ref, ofx_ref, wosb_ref, wofx_ref, wout_ref, o_ref):
    x = x_ref[...]
    h = _rms_norm_rows(x, g_ref[...]).astype(BF16)
    gates = jax.nn.sigmoid(_dot(h, wg_ref[...]) + bg_ref[...])
    y_sb = _dot(osb_ref[...], wosb_ref[...])
    y_fx = _dot(ofx_ref[...], wofx_ref[...])
    mixed = gates[:, :D_MODEL] * y_sb + gates[:, D_MODEL:] * y_fx
    o_ref[...] = x + _dot(mixed.astype(BF16), wout_ref[...])


def _out_proj(x2d, g_mix, w_gate, b_gate, o_sb, o_fx, w_o_sb, w_o_fox, w_out):
    m = x2d.shape[0]
    tm = ROW_TILE
    const = lambda i: (0, 0)
    row = lambda i: (i, 0)
    full = lambda a: pl.BlockSpec(a.shape, const)
    return pl.pallas_call(
        _out_proj_kernel,
        grid=(m // tm,),
        in_specs=[pl.BlockSpec((tm, D_MODEL), row), full(g_mix), full(w_gate), full(b_gate),
                  pl.BlockSpec((tm, SB_WIDTH), row), pl.BlockSpec((tm, FOX_WIDTH), row),
                  full(w_o_sb), full(w_o_fox), full(w_out)],
        out_specs=pl.BlockSpec((tm, D_MODEL), row),
        out_shape=jax.ShapeDtypeStruct((m, D_MODEL), F32),
        compiler_params=pltpu.CompilerParams(dimension_semantics=("arbitrary",),
                                             vmem_limit_bytes=VMEM_LIMIT),
        name="out_proj",
    )(x2d, g_mix, w_gate, b_gate, o_sb, o_fx, w_o_sb, w_o_fox, w_out)


def _conv_ffn_kernel(x_ref, g_ref, wu_ref, cw_ref, cb_ref, wd_ref, o_ref, hext_ref, prev_ref, u0_ref, u1_ref,
                     act_ref, *, tiles_per_seq):
    tm, tf, halo = FFN_ROW_TILE, FFN_COL_TILE, BF16_ROWS
    nf = D_FF // tf
    h = _rms_norm_rows(x_ref[...], g_ref[...]).astype(BF16)
    first = pl.program_id(0) % tiles_per_seq == 0
    hext_ref[:halo, :] = jnp.where(first, jnp.zeros_like(prev_ref), prev_ref[...])
    hext_ref[halo:, :] = h
    prev_ref[...] = h[tm - halo:, :]
    o_ref[...] = x_ref[...]

    u_refs = (u0_ref, u1_ref)

    def cols(f, part):
        return pl.ds(pl.multiple_of(part * D_FF + f * tf, tf), tf)

    def up(f, slot):
        w = jnp.concatenate([wu_ref[:, cols(f, 0)], wu_ref[:, cols(f, 1)]], axis=1)
        u_refs[slot][...] = _dot(hext_ref[...], w)

    def act(f, slot, half):
        u = u_refs[slot]
        cw = jnp.concatenate([cw_ref[:, cols(f, 0)], cw_ref[:, cols(f, 1)]], axis=1)
        cb = jnp.concatenate([cb_ref[:, cols(f, 0)], cb_ref[:, cols(f, 1)]], axis=1)
        for r0 in range(0, tm, FFN_ACT_ROWS):
            y = cb + cw[2:3, :] * u[pl.ds(halo + r0, FFN_ACT_ROWS), :]
            y = y + cw[1:2, :] * u[pl.ds(halo + r0 - 1, FFN_ACT_ROWS), :]
            y = y + cw[0:1, :] * u[pl.ds(halo + r0 - 2, FFN_ACT_ROWS), :]
            gate, val = y[:, :tf], y[:, tf:]
            act_ref[r0:r0 + FFN_ACT_ROWS, half * tf:(half + 1) * tf] = (
                gate * jax.nn.sigmoid(gate) * val).astype(BF16)

    def down(f0, n_tiles):
        rows = pl.ds(pl.multiple_of(f0 * tf, tf), tf * n_tiles)
        o_ref[...] += _dot(act_ref[:, :tf * n_tiles], wd_ref[rows, :])

    assert nf % 2 == 1
    up(0, 0)

    def body(p, _):
        f = 2 * p
        up(f + 1, 1)
        act(f, 0, 0)
        up(f + 2, 0)
        act(f + 1, 1, 1)
        down(f, 2)
        return 0

    lax.fori_loop(0, nf // 2, body, 0)
    act(nf - 1, 0, 0)
    down(nf - 1, 1)


def _conv_ffn(x2d, g_ffn, w_up, conv_w, conv_b, w_down, *, seq):
    m = x2d.shape[0]
    tm, tf = FFN_ROW_TILE, FFN_COL_TILE
    row = lambda i: (i, 0)
    resident = lambda a: pl.BlockSpec(a.shape, lambda i: (0, 0), pipeline_mode=pl.Buffered(1))
    return pl.pallas_call(
        functools.partial(_conv_ffn_kernel, tiles_per_seq=seq // tm),
        grid=(m // tm,),
        in_specs=[pl.BlockSpec((tm, D_MODEL), row), resident(g_ffn), resident(w_up), resident(conv_w),
                  resident(conv_b), resident(w_down)],
        out_specs=pl.BlockSpec((tm, D_MODEL), row),
        out_shape=jax.ShapeDtypeStruct((m, D_MODEL), F32),
        scratch_shapes=[pltpu.VMEM((tm + BF16_ROWS, D_MODEL), BF16), pltpu.VMEM((BF16_ROWS, D_MODEL), BF16),
                        pltpu.VMEM((tm + BF16_ROWS, 2 * tf), F32), pltpu.VMEM((tm + BF16_ROWS, 2 * tf), F32),
                        pltpu.VMEM((tm, 2 * tf), BF16)],
        compiler_params=pltpu.CompilerParams(dimension_semantics=("arbitrary",),
                                             vmem_limit_bytes=VMEM_LIMIT),
        name="conv_ffn",
    )(x2d, g_ffn, w_up, conv_w, conv_b, w_down)


def kernel(x, g_mix, w_in, b_forget, b_gate, g_q, g_k, w_o_sb, w_o_fox, w_out,
           g_ffn, w_up, conv_w, conv_b, w_down):
    batch, seq, _ = x.shape
    depth = g_mix.shape[0]
    assert seq % Q_TILE == 0 and seq % FFN_ROW_TILE == 0 and seq % ROW_TILE == 0
    x2d = x.reshape(batch * seq, D_MODEL)
    head_of_lane = jnp.arange(FOX_WIDTH) // HEAD_DIM
    seg = jnp.where(head_of_lane[:, None] == head_of_lane[None, :], 1.0 / HEAD_DIM, 0.0).astype(BF16)
    row2d = lambda a: a.reshape(1, -1)
    for layer in range(depth):
        w = w_in[layer]
        w_f = jnp.pad(w[:, OFF_FORGET:OFF_GATE].astype(BF16), ((0, 0), (0, LANES - N_FOX_HEADS)))
        b_f = jnp.pad(b_forget[layer], (0, LANES - N_FOX_HEADS)).reshape(1, LANES)
        q_sb, k_sb, v_sb, q_fx, k_fx, v_fx, c_row = _in_proj(
            x2d, row2d(g_mix[layer]), w[:, :OFF_FOX].astype(BF16), w[:, OFF_FOX:OFF_FORGET].astype(BF16), w_f, b_f,
            row2d(jnp.tile(g_q[layer], N_FOX_HEADS)), row2d(jnp.tile(g_k[layer], N_FOX_HEADS)), seg, seq=seq)
        o_sb = _sb_attention(q_sb, k_sb, v_sb, batch=batch, seq=seq)
        o_fx = _fox_attention(q_fx, k_fx, v_fx, c_row, batch=batch, seq=seq)
        x2d = _out_proj(x2d, row2d(g_mix[layer]), w[:, OFF_GATE:].astype(BF16), row2d(b_gate[layer]), o_sb, o_fx,
                        w_o_sb[layer].astype(BF16), w_o_fox[layer].astype(BF16), w_out[layer].astype(BF16))
        x2d = _conv_ffn(x2d, row2d(g_ffn[layer]), w_up[layer].astype(BF16), conv_w[layer],
                        row2d(conv_b[layer]), w_down[layer].astype(BF16), seq=seq)
    return x2d.reshape(batch, seq, D_MODEL)
```

```python
import functools

import jax
import jax.numpy as jnp
from jax import lax
from jax.experimental import pallas as pl
from jax.experimental.pallas import tpu as pltpu

D_MODEL = 1024
HEAD_DIM = 64
N_SB_HEADS = 8
N_FOX_HEADS = 8
SB_WIDTH = N_SB_HEADS * HEAD_DIM
FOX_WIDTH = N_FOX_HEADS * HEAD_DIM
D_FF = 2816
CONV_WIDTH = 3
EPS = 1e-6
OFF_FOX = 3 * SB_WIDTH
OFF_FORGET = OFF_FOX + 3 * FOX_WIDTH
OFF_GATE = OFF_FORGET + N_FOX_HEADS
N_BRANCHES = 2

LANES = 128
HEADS_PER_BLOCK = LANES // HEAD_DIM
BF16_ROWS = 16
NEG_BIG = -1e30
LOG2E = 1.4426950408889634
SB_UNDERFLOW_LOG2 = -160.0

F32 = jnp.float32
BF16 = jnp.bfloat16

ROW_TILE = 512
FFN_ROW_TILE = 1024
FFN_COL_TILE = 256
FFN_ACT_ROWS = 128
Q_TILE = 512
SB_KEY_CHUNK = 256
FOX_KEY_CHUNK = 512
FOX_PAIR_WIDTHS = (1024, 512)
FOX_SOLO_WIDTHS = (2048, 1024, 512)
FOX_ZERO_LOG2 = 151.0
FOX_NORM_SLACK = 1.05
VMEM_LIMIT = 56 * 1024 * 1024


def _dot(a, b):
    return jnp.dot(a, b, preferred_element_type=F32)


def _dot_nt(a, b):
    return lax.dot_general(a, b, (((1,), (1,)), ((), ())), preferred_element_type=F32)


def _rms_norm_rows(x, g):
    return x * lax.rsqrt(jnp.mean(x * x, axis=-1, keepdims=True) + EPS) * g


def _cumsum_rows(x):
    n = x.shape[0]
    row = lax.broadcasted_iota(jnp.int32, x.shape, 0)
    step = 1
    while step < n:
        x = x + jnp.where(row >= step, pltpu.roll(x, step, axis=0), 0.0)
        step *= 2
    return x


def _in_proj_kernel(x_ref, g_ref, wsb32_ref, wfx32_ref, wf32_ref, bf_ref, gq_ref, gk_ref, seg_ref,
                    qsb_ref, ksb_ref, vsb_ref, qfx_ref, kfx_ref, vfx_ref, crow_ref,
                    carry_ref, wsb_ref, wfx_ref, wf_ref, *, tiles_per_seq):
    @pl.when(pl.program_id(0) == 0)
    def _():
        wsb_ref[...] = wsb32_ref[...].T.astype(BF16)
        wfx_ref[...] = wfx32_ref[...].T.astype(BF16)
        wf_ref[...] = wf32_ref[...].T.astype(BF16)

    scale = HEAD_DIM ** -0.5 * LOG2E
    h = _rms_norm_rows(x_ref[...], g_ref[...]).astype(BF16)

    psb = _dot(h, wsb_ref[...])
    qsb_ref[...] = (psb[:, :SB_WIDTH] * scale).astype(BF16)
    ksb_ref[...] = psb[:, SB_WIDTH:2 * SB_WIDTH].astype(BF16)
    vsb_ref[...] = psb[:, 2 * SB_WIDTH:].astype(BF16)

    pfx = _dot(h, wfx_ref[...])
    seg = seg_ref[...]

    def head_norm(t, g):
        ms = _dot((t * t).astype(BF16), seg)
        return t * lax.rsqrt(ms + EPS) * g

    qfx_ref[...] = (head_norm(pfx[:, :FOX_WIDTH], gq_ref[...]) * scale).astype(BF16)
    kfx_ref[...] = head_norm(pfx[:, FOX_WIDTH:2 * FOX_WIDTH], gk_ref[...]).astype(BF16)
    vfx_ref[...] = pfx[:, 2 * FOX_WIDTH:].astype(BF16)

    @pl.when(pl.program_id(0) % tiles_per_seq == 0)
    def _():
        carry_ref[...] = jnp.zeros_like(carry_ref)

    log_f = jax.nn.log_sigmoid(_dot(h, wf_ref[...]) + bf_ref[...])
    c = _cumsum_rows(log_f) + carry_ref[0:1, :]
    carry_ref[0:1, :] = c[-1:, :]
    crow_ref[...] = (c * LOG2E).T[:N_FOX_HEADS, :]


def _in_proj(x2d, g_mix, w_in_t, b_f, g_q, g_k, seg, *, seq):
    m = x2d.shape[0]
    tm = ROW_TILE
    qkv = 3 * SB_WIDTH
    assert OFF_FOX == qkv and OFF_FORGET == 2 * qkv and OFF_FORGET % LANES == 0
    const = lambda i: (0, 0)
    row = lambda i: (i, 0)
    full = lambda a: pl.BlockSpec(a.shape, const)
    once = pl.Buffered(1)
    out_w = lambda: pl.BlockSpec((tm, SB_WIDTH), row)
    return pl.pallas_call(
        functools.partial(_in_proj_kernel, tiles_per_seq=seq // tm),
        grid=(m // tm,),
        in_specs=[pl.BlockSpec((tm, D_MODEL), row), full(g_mix),
                  pl.BlockSpec((qkv, D_MODEL), lambda i: (0, 0), pipeline_mode=once),
                  pl.BlockSpec((qkv, D_MODEL), lambda i: (1, 0), pipeline_mode=once),
                  pl.BlockSpec((LANES, D_MODEL), lambda i: (OFF_FORGET // LANES, 0), pipeline_mode=once),
                  full(b_f), full(g_q), full(g_k), full(seg)],
        out_specs=[out_w(), out_w(), out_w(), out_w(), out_w(), out_w(),
                   pl.BlockSpec((N_FOX_HEADS, tm), lambda i: (0, i))],
        out_shape=[jax.ShapeDtypeStruct((m, SB_WIDTH), BF16)] * 6
        + [jax.ShapeDtypeStruct((N_FOX_HEADS, m), F32)],
        scratch_shapes=[pltpu.VMEM((8, LANES), F32), pltpu.VMEM((D_MODEL, qkv), BF16),
                        pltpu.VMEM((D_MODEL, qkv), BF16), pltpu.VMEM((D_MODEL, LANES), BF16)],
        compiler_params=pltpu.CompilerParams(dimension_semantics=("arbitrary",),
                                             vmem_limit_bytes=VMEM_LIMIT),
        name="in_proj",
    )(x2d, g_mix, w_in_t, w_in_t, w_in_t, b_f, g_q, g_k, seg)


def _head_lane_masks():
    lane = lax.broadcasted_iota(jnp.int32, (1, LANES), 1)
    return [(lane >= hh * HEAD_DIM) & (lane < (hh + 1) * HEAD_DIM) for hh in range(HEADS_PER_BLOCK)]


def _neg_abs(x):
    bits = lax.bitcast_convert_type(x, jnp.uint32) | jnp.uint32(0x80000000)
    return lax.bitcast_convert_type(bits, F32)


def _sb_weights(q, km, carry, later_sum, strict):
    z = _dot_nt(q, km)
    sp = jnp.maximum(z, 0.0) + jnp.log2(1.0 + jnp.exp2(_neg_abs(z)))
    if strict is not None:
        sp = jnp.where(strict, sp, 0.0)
    later = _dot(sp.astype(BF16), later_sum)
    w = jnp.exp2((z - sp) + later + jnp.concatenate([carry] * (z.shape[1] // LANES), axis=1))
    if strict is not None:
        w = jnp.where(strict, w, 0.0)
    total = later[:, :1] - sp[:, :1]
    return w.astype(BF16), carry + jnp.broadcast_to(total, carry.shape)


def _sb_kernel(q_ref, k_ref, v_ref, o_ref, acc_ref, carry_ref):
    tq, kc = Q_TILE, SB_KEY_CHUNK
    q0 = pl.program_id(2) * tq
    lanes = _head_lane_masks()
    later_sum = jnp.where(lax.broadcasted_iota(jnp.int32, (kc, kc), 0) > lax.broadcasted_iota(jnp.int32, (kc, kc), 1),
                          -1.0, 0.0).astype(BF16)
    acc_ref[...] = jnp.zeros_like(acc_ref)
    carry_ref[...] = jnp.zeros_like(carry_ref)

    def step(r0, r1, k0, strict):
        k_chunk = k_ref[pl.ds(k0, kc), :]
        v_chunk = v_ref[pl.ds(k0, kc), :]
        zero = jnp.zeros_like(k_chunk)
        ws, vms = [], []
        for hh in range(HEADS_PER_BLOCK):
            w, carry = _sb_weights(q_ref[r0:r1, :], jnp.where(lanes[hh], k_chunk, zero), carry_ref[hh, r0:r1, :],
                                   later_sum, strict)
            carry_ref[hh, r0:r1, :] = carry
            ws.append(w)
            vms.append(jnp.where(lanes[hh], v_chunk, zero))
        acc_ref[r0:r1, :] += _dot(jnp.concatenate(ws, axis=1), jnp.concatenate(vms, axis=0))

    def live(r0, r1):
        return jnp.max(carry_ref[:, r0:r1, :]) > SB_UNDERFLOW_LOG2

    for jd in reversed(range(tq // kc)):
        r0 = jd * kc
        qpos = lax.broadcasted_iota(jnp.int32, (tq - r0, kc), 0)
        kpos = lax.broadcasted_iota(jnp.int32, (tq - r0, kc), 1)
        step(r0, tq, pl.multiple_of(q0 + r0, kc), kpos < qpos)

    has_prev = q0 > 0
    k_prev = pl.multiple_of(jnp.maximum(q0 - kc, 0), kc)
    step(0, kc, k_prev, jnp.broadcast_to(has_prev, (kc, kc)))

    @pl.when(has_prev & live(0, tq))
    def _():
        for r0 in range(kc, tq, kc):
            @pl.when(live(r0, r0 + kc))
            def _():
                step(r0, r0 + kc, k_prev, None)

        def body(state):
            it, _ = state
            step(0, tq, pl.multiple_of(q0 - (it + 1) * kc, kc), None)
            return it + 1, live(0, tq).astype(jnp.int32)

        lax.while_loop(lambda s: (s[0] < q0 // kc) & (s[1] > 0), body,
                       (jnp.int32(1), live(0, tq).astype(jnp.int32)))
    o_ref[...] = acc_ref[...].astype(BF16)


def _sb_attention(q, k, v, *, batch, seq):
    m = q.shape[0]
    nq = seq // Q_TILE
    n_blocks = SB_WIDTH // LANES
    q_spec = pl.BlockSpec((Q_TILE, LANES), lambda b, hp, qi: (b * nq + qi, hp))
    kv_spec = pl.BlockSpec((seq, LANES), lambda b, hp, qi: (b, hp))
    return pl.pallas_call(
        _sb_kernel,
        grid=(batch, n_blocks, nq),
        in_specs=[q_spec, kv_spec, kv_spec],
        out_specs=q_spec,
        out_shape=jax.ShapeDtypeStruct((m, SB_WIDTH), BF16),
        scratch_shapes=[pltpu.VMEM((Q_TILE, LANES), F32),
                        pltpu.VMEM((HEADS_PER_BLOCK, Q_TILE, LANES), F32)],
        compiler_params=pltpu.CompilerParams(dimension_semantics=("arbitrary",) * 3,
                                             vmem_limit_bytes=VMEM_LIMIT),
        name="sb_attn",
    )(q, k, v)


def _fox_kernel(cend_ref, thr_ref, q_ref, k_ref, v_ref, crow_ref, o_ref, z_ref, m_ref, acc_ref):
    tq, kc = Q_TILE, FOX_KEY_CHUNK
    b, hp, qi = pl.program_id(0), pl.program_id(1), pl.program_id(2)
    nq = pl.num_programs(2)
    q0 = qi * tq
    lane = lax.broadcasted_iota(jnp.int32, (1, LANES), 1)
    sub = lax.broadcasted_iota(jnp.int32, (N_FOX_HEADS, 1), 0)
    both = tuple(range(HEADS_PER_BLOCK))
    m_ref[...] = jnp.full_like(m_ref, NEG_BIG)

    def head_lanes(hh):
        return (lane >= hh * HEAD_DIM) & (lane < (hh + 1) * HEAD_DIM)

    def first_visible_chunk(hh):
        head = hp * HEADS_PER_BLOCK + hh
        base = cend_ref[head, b * nq + jnp.maximum(qi - 1, 0)]

        def body(i, state):
            first, reach = state
            c = qi - 1 - i
            reach = reach & (cend_ref[head, b * nq + c] - base <= thr_ref[0, 0]).astype(jnp.int32)
            return jnp.where(reach > 0, c, first), reach

        return lax.fori_loop(0, qi, body, (qi, jnp.int32(1)))[0]

    def logits(k0, width, heads, causal):
        k_chunk = k_ref[pl.ds(k0, width), :]
        c_rows = crow_ref[:, pl.ds(k0, width)]
        zero = jnp.zeros_like(k_chunk)
        for hh in heads:
            ck = jnp.sum(jnp.where(sub == hp * HEADS_PER_BLOCK + hh, c_rows, 0.0), axis=0, keepdims=True)
            z = _dot_nt(q_ref[...], jnp.where(head_lanes(hh), k_chunk, zero)) - ck
            if causal:
                qpos = lax.broadcasted_iota(jnp.int32, (tq, width), 0)
                kpos = lax.broadcasted_iota(jnp.int32, (tq, width), 1)
                z = jnp.where(kpos <= qpos, z, NEG_BIG)
            z_ref[hh, :, pl.ds(k0, width)] = z
            mx = m_ref[hh]
            for c in range(width // LANES):
                mx = jnp.maximum(mx, z[:, c * LANES:(c + 1) * LANES])
            m_ref[hh] = mx

    def weigh(k0, width, heads):
        v_chunk = v_ref[pl.ds(k0, width), :]
        one = jnp.ones_like(v_chunk)
        for hh in heads:
            row_max = m_ref[hh]
            p = [jnp.exp2(z_ref[hh, :, pl.ds(k0 + c * LANES, LANES)] - row_max) for c in range(width // LANES)]
            acc_ref[hh] += _dot(jnp.concatenate(p, axis=1).astype(BF16), jnp.where(head_lanes(hh), v_chunk, one))

    def walk(start, stop, widths, fn):
        n_wide = (stop - start) // widths[0]

        def body(j, _):
            fn(pl.multiple_of(start + j * widths[0], kc), widths[0])
            return 0

        lax.fori_loop(0, n_wide, body, 0)
        pos = start + n_wide * widths[0]
        for width in widths[1:]:
            take = stop - pos >= width

            @pl.when(take)
            def _():
                fn(pl.multiple_of(pos, kc), width)

            pos = pos + jnp.where(take, width, 0)

    first = [first_visible_chunk(hh) for hh in both]
    slow = jnp.where(first[0] <= first[1], 0, 1)
    k_lo, k_hi = jnp.minimum(first[0], first[1]) * kc, jnp.maximum(first[0], first[1]) * kc

    walk(k_lo, k_hi, FOX_SOLO_WIDTHS, lambda k0, width: logits(k0, width, (slow,), False))
    walk(k_hi, q0, FOX_PAIR_WIDTHS, lambda k0, width: logits(k0, width, both, False))
    logits(pl.multiple_of(q0, kc), kc, both, True)

    for hh in both:
        m_ref[hh] = jnp.broadcast_to(jnp.max(m_ref[hh], axis=1, keepdims=True), (tq, LANES))
    acc_ref[...] = jnp.zeros_like(acc_ref)
    walk(k_lo, k_hi, FOX_SOLO_WIDTHS, lambda k0, width: weigh(k0, width, (slow,)))
    walk(k_hi, q0 + tq, FOX_PAIR_WIDTHS, lambda k0, width: weigh(k0, width, both))

    out = jnp.zeros((tq, LANES), F32)
    for hh in range(HEADS_PER_BLOCK):
        acc = acc_ref[hh]
        out = out + jnp.where(head_lanes(hh), acc / pltpu.roll(acc, HEAD_DIM, axis=1), 0.0)
    o_ref[...] = out.astype(BF16)


def _fox_attention(q, k, v, c_row, qk_norm_bound, *, batch, seq):
    assert Q_TILE == FOX_KEY_CHUNK
    m = q.shape[0]
    nq = seq // Q_TILE
    n_blocks = FOX_WIDTH // LANES
    q_spec = pl.BlockSpec((Q_TILE, LANES), lambda b, hp, qi: (b * nq + qi, hp))
    kv_spec = pl.BlockSpec((seq, LANES), lambda b, hp, qi: (b, hp))
    c_end = c_row[:, FOX_KEY_CHUNK - 1::FOX_KEY_CHUNK]
    thr = (FOX_ZERO_LOG2 + 2.0 * qk_norm_bound).reshape(1, 1).astype(F32)
    smem = pl.BlockSpec(memory_space=pltpu.SMEM)
    return pl.pallas_call(
        _fox_kernel,
        grid=(batch, n_blocks, nq),
        in_specs=[smem, smem, q_spec, kv_spec, kv_spec,
                  pl.BlockSpec((N_FOX_HEADS, seq), lambda b, hp, qi: (0, b))],
        out_specs=q_spec,
        out_shape=jax.ShapeDtypeStruct((m, FOX_WIDTH), BF16),
        scratch_shapes=[pltpu.VMEM((HEADS_PER_BLOCK, Q_TILE, seq), F32),
                        pltpu.VMEM((HEADS_PER_BLOCK, Q_TILE, LANES), F32),
                        pltpu.VMEM((HEADS_PER_BLOCK, Q_TILE, LANES), F32)],
        compiler_params=pltpu.CompilerParams(dimension_semantics=("arbitrary",) * 3,
                                             vmem_limit_bytes=VMEM_LIMIT),
        name="fox_attn",
    )(c_end, thr, q, k, v, c_row)


def _out_proj_kernel(x_ref, g_ref, wga_ref, wgb_ref, wgc_ref, bg_ref, osb_ref, ofx_ref, wosb32_ref, wofx32_ref,
                     wout32_ref, o_ref, wg_ref, wosb_ref, wofx_ref, wout_ref):
    @pl.when(pl.program_id(0) == 0)
    def _():
        skip = OFF_GATE % D_MODEL
        gate_rows = jnp.concatenate([wga_ref[skip:, :], wgb_ref[...], wgc_ref[...]], axis=0)
        wg_ref[...] = gate_rows.T.astype(BF16)
        wosb_ref[...] = wosb32_ref[...].astype(BF16)
        wofx_ref[...] = wofx32_ref[...].astype(BF16)
        wout_ref[...] = wout32_ref[...].astype(BF16)

    x = x_ref[...]
    h = _rms_norm_rows(x, g_ref[...]).astype(BF16)
    gates = jax.nn.sigmoid(_dot(h, wg_ref[...]) + bg_ref[...])
    y_sb = _dot(osb_ref[...], wosb_ref[...])
    y_fx = _dot(ofx_ref[...], wofx_ref[...])
    mixed = gates[:, :D_MODEL] * y_sb + gates[:, D_MODEL:] * y_fx
    o_ref[...] = x + _dot(mixed.astype(BF16), wout_ref[...])


def _out_proj(x2d, g_mix, w_in_t, b_gate, o_sb, o_fx, w_o_sb, w_o_fox, w_out):
    m = x2d.shape[0]
    tm = ROW_TILE
    n_gate = N_BRANCHES * D_MODEL
    skip = OFF_GATE % D_MODEL
    assert skip % 8 == 0 and w_in_t.shape[0] == OFF_GATE + n_gate
    const = lambda i: (0, 0)
    row = lambda i: (i, 0)
    full = lambda a: pl.BlockSpec(a.shape, const)
    once = pl.Buffered(1)
    resident = lambda a: pl.BlockSpec(a.shape, const, pipeline_mode=once)
    blk_a = OFF_GATE // D_MODEL
    return pl.pallas_call(
        _out_proj_kernel,
        grid=(m // tm,),
        in_specs=[pl.BlockSpec((tm, D_MODEL), row), full(g_mix),
                  pl.BlockSpec((D_MODEL, D_MODEL), lambda i: (blk_a, 0), pipeline_mode=once),
                  pl.BlockSpec((D_MODEL, D_MODEL), lambda i: (blk_a + 1, 0), pipeline_mode=once),
                  pl.BlockSpec((skip, D_MODEL), lambda i: ((blk_a + 2) * D_MODEL // skip, 0), pipeline_mode=once),
                  full(b_gate),
                  pl.BlockSpec((tm, SB_WIDTH), row), pl.BlockSpec((tm, FOX_WIDTH), row),
                  resident(w_o_sb), resident(w_o_fox), resident(w_out)],
        out_specs=pl.BlockSpec((tm, D_MODEL), row),
        out_shape=jax.ShapeDtypeStruct((m, D_MODEL), F32),
        scratch_shapes=[pltpu.VMEM((D_MODEL, n_gate), BF16), pltpu.VMEM(w_o_sb.shape, BF16),
                        pltpu.VMEM(w_o_fox.shape, BF16), pltpu.VMEM(w_out.shape, BF16)],
        compiler_params=pltpu.CompilerParams(dimension_semantics=("arbitrary",),
                                             vmem_limit_bytes=VMEM_LIMIT),
        name="out_proj",
    )(x2d, g_mix, w_in_t, w_in_t, w_in_t, b_gate, o_sb, o_fx, w_o_sb, w_o_fox, w_out)


def _conv_ffn_kernel(x_ref, g_ref, wu_ref, cw_ref, cb_ref, wd_ref, o_ref, hext_ref, prev_ref, u0_ref, u1_ref,
                     act_ref, *, tiles_per_seq):
    tm, tf, halo = FFN_ROW_TILE, FFN_COL_TILE, BF16_ROWS
    nf = D_FF // tf
    h = _rms_norm_rows(x_ref[...], g_ref[...]).astype(BF16)
    first = pl.program_id(0) % tiles_per_seq == 0
    hext_ref[:halo, :] = jnp.where(first, jnp.zeros_like(prev_ref), prev_ref[...])
    hext_ref[halo:, :] = h
    prev_ref[...] = h[tm - halo:, :]
    o_ref[...] = x_ref[...]

    u_refs = (u0_ref, u1_ref)

    def cols(f, part):
        return pl.ds(pl.multiple_of(part * D_FF + f * tf, tf), tf)

    def up(f, slot):
        w = jnp.concatenate([wu_ref[:, cols(f, 0)], wu_ref[:, cols(f, 1)]], axis=1)
        u_refs[slot][...] = _dot(hext_ref[...], w)

    def act(f, slot, half):
        u = u_refs[slot]
        cw = jnp.concatenate([cw_ref[:, cols(f, 0)], cw_ref[:, cols(f, 1)]], axis=1)
        cb = jnp.concatenate([cb_ref[:, cols(f, 0)], cb_ref[:, cols(f, 1)]], axis=1)
        for r0 in range(0, tm, FFN_ACT_ROWS):
            y = cb + cw[2:3, :] * u[pl.ds(halo + r0, FFN_ACT_ROWS), :]
            y = y + cw[1:2, :] * u[pl.ds(halo + r0 - 1, FFN_ACT_ROWS), :]
            y = y + cw[0:1, :] * u[pl.ds(halo + r0 - 2, FFN_ACT_ROWS), :]
            gate, val = y[:, :tf], y[:, tf:]
            act_ref[r0:r0 + FFN_ACT_ROWS, half * tf:(half + 1) * tf] = (
                gate * jax.nn.sigmoid(gate) * val).astype(BF16)

    def down(f0, n_tiles):
        rows = pl.ds(pl.multiple_of(f0 * tf, tf), tf * n_tiles)
        o_ref[...] += _dot(act_ref[:, :tf * n_tiles], wd_ref[rows, :])

    assert nf % 2 == 1
    up(0, 0)

    def body(p, _):
        f = 2 * p
        up(f + 1, 1)
        act(f, 0, 0)
        up(f + 2, 0)
        act(f + 1, 1, 1)
        down(f, 2)
        return 0

    lax.fori_loop(0, nf // 2, body, 0)
    act(nf - 1, 0, 0)
    down(nf - 1, 1)


def _conv_ffn(x2d, g_ffn, w_up, conv_w, conv_b, w_down, *, seq):
    m = x2d.shape[0]
    tm, tf = FFN_ROW_TILE, FFN_COL_TILE
    row = lambda i: (i, 0)
    resident = lambda a: pl.BlockSpec(a.shape, lambda i: (0, 0), pipeline_mode=pl.Buffered(1))
    return pl.pallas_call(
        functools.partial(_conv_ffn_kernel, tiles_per_seq=seq // tm),
        grid=(m // tm,),
        in_specs=[pl.BlockSpec((tm, D_MODEL), row), resident(g_ffn), resident(w_up), resident(conv_w),
                  resident(conv_b), resident(w_down)],
        out_specs=pl.BlockSpec((tm, D_MODEL), row),
        out_shape=jax.ShapeDtypeStruct((m, D_MODEL), F32),
        scratch_shapes=[pltpu.VMEM((tm + BF16_ROWS, D_MODEL), BF16), pltpu.VMEM((BF16_ROWS, D_MODEL), BF16),
                        pltpu.VMEM((tm + BF16_ROWS, 2 * tf), F32), pltpu.VMEM((tm + BF16_ROWS, 2 * tf), F32),
                        pltpu.VMEM((tm, 2 * tf), BF16)],
        compiler_params=pltpu.CompilerParams(dimension_semantics=("arbitrary",),
                                             vmem_limit_bytes=VMEM_LIMIT),
        name="conv_ffn",
    )(x2d, g_ffn, w_up, conv_w, conv_b, w_down)


def kernel(x, g_mix, w_in, b_forget, b_gate, g_q, g_k, w_o_sb, w_o_fox, w_out,
           g_ffn, w_up, conv_w, conv_b, w_down):
    batch, seq, _ = x.shape
    depth = g_mix.shape[0]
    assert seq % Q_TILE == 0 and seq % FFN_ROW_TILE == 0 and seq % ROW_TILE == 0
    x2d = x.reshape(batch * seq, D_MODEL)
    head_of_lane = jnp.arange(FOX_WIDTH) // HEAD_DIM
    seg = jnp.where(head_of_lane[:, None] == head_of_lane[None, :], 1.0 / HEAD_DIM, 0.0).astype(BF16)
    row2d = lambda a: a.reshape(1, -1)
    for layer in range(depth):
        w_t = w_in[layer].T
        b_f = jnp.pad(b_forget[layer], (0, LANES - N_FOX_HEADS)).reshape(1, LANES)
        q_sb, k_sb, v_sb, q_fx, k_fx, v_fx, c_row = _in_proj(
            x2d, row2d(g_mix[layer]), w_t, b_f,
            row2d(jnp.tile(g_q[layer], N_FOX_HEADS)), row2d(jnp.tile(g_k[layer], N_FOX_HEADS)), seg, seq=seq)
        o_sb = _sb_attention(q_sb, k_sb, v_sb, batch=batch, seq=seq)
        qk_bound = (FOX_NORM_SLACK * HEAD_DIM ** 0.5 * LOG2E
                    * jnp.max(jnp.abs(g_q[layer])) * jnp.max(jnp.abs(g_k[layer])))
        o_fx = _fox_attention(q_fx, k_fx, v_fx, c_row, qk_bound, batch=batch, seq=seq)
        x2d = _out_proj(x2d, row2d(g_mix[layer]), w_t, row2d(b_gate[layer]), o_sb, o_fx,
                        w_o_sb[layer], w_o_fox[layer], w_out[layer])
        x2d = _conv_ffn(x2d, row2d(g_ffn[layer]), w_up[layer].astype(BF16), conv_w[layer],
                        row2d(conv_b[layer]), w_down[layer].astype(BF16), seq=seq)
    return x2d.reshape(batch, seq, D_MODEL)
```

```python
import functools

import jax
import jax.numpy as jnp
from jax import lax
from jax.experimental import pallas as pl
from jax.experimental.pallas import tpu as pltpu

D_MODEL = 1024
HEAD_DIM = 64
N_SB_HEADS = 8
N_FOX_HEADS = 8
SB_WIDTH = N_SB_HEADS * HEAD_DIM
FOX_WIDTH = N_FOX_HEADS * HEAD_DIM
D_FF = 2816
CONV_WIDTH = 3
EPS = 1e-6
OFF_FOX = 3 * SB_WIDTH
OFF_FORGET = OFF_FOX + 3 * FOX_WIDTH
OFF_GATE = OFF_FORGET + N_FOX_HEADS
N_BRANCHES = 2

LANES = 128
HEADS_PER_BLOCK = LANES // HEAD_DIM
BF16_ROWS = 16
NEG_BIG = -1e30
LOG2E = 1.4426950408889634
SB_UNDERFLOW_LOG2 = -160.0

F32 = jnp.float32
BF16 = jnp.bfloat16

ROW_TILE = 512
FFN_ROW_TILE = 1024
FFN_COL_TILE = 256
FFN_ACT_ROWS = 128
Q_TILE = 512
SB_KEY_CHUNK = 256
SB_TILES_PER_STEP = 4
FOX_KEY_CHUNK = 512
FOX_TILES_PER_STEP = 2
FOX_PAIR_WIDTHS = (1024, 512)
FOX_SOLO_WIDTHS = (2048, 1024, 512)
FOX_ZERO_LOG2 = 151.0
FOX_NORM_SLACK = 1.05
VMEM_LIMIT = 56 * 1024 * 1024


def _dot(a, b):
    return jnp.dot(a, b, preferred_element_type=F32)


def _dot_nt(a, b):
    return lax.dot_general(a, b, (((1,), (1,)), ((), ())), preferred_element_type=F32)


def _rms_norm_rows(x, g):
    return x * lax.rsqrt(jnp.mean(x * x, axis=-1, keepdims=True) + EPS) * g


def _cumsum_rows(x):
    n = x.shape[0]
    row = lax.broadcasted_iota(jnp.int32, x.shape, 0)
    step = 1
    while step < n:
        x = x + jnp.where(row >= step, pltpu.roll(x, step, axis=0), 0.0)
        step *= 2
    return x


def _in_proj_kernel(x_ref, g_ref, wsb32_ref, wfx32_ref, wf32_ref, bf_ref, gq_ref, gk_ref, seg_ref,
                    qsb_ref, ksb_ref, vsb_ref, qfx_ref, kfx_ref, vfx_ref, crow_ref,
                    carry_ref, wsb_ref, wfx_ref, wf_ref, *, tiles_per_seq):
    @pl.when(pl.program_id(0) == 0)
    def _():
        wsb_ref[...] = wsb32_ref[...].T.astype(BF16)
        wfx_ref[...] = wfx32_ref[...].T.astype(BF16)
        wf_ref[...] = wf32_ref[...].T.astype(BF16)

    scale = HEAD_DIM ** -0.5 * LOG2E
    h = _rms_norm_rows(x_ref[...], g_ref[...]).astype(BF16)

    @pl.when(pl.program_id(0) % tiles_per_seq == 0)
    def _():
        carry_ref[...] = jnp.zeros_like(carry_ref)

    log_f = jax.nn.log_sigmoid(_dot(h, wf_ref[...]) + bf_ref[...])
    c = _cumsum_rows(log_f) + carry_ref[0:1, :]
    carry_ref[0:1, :] = c[-1:, :]
    crow_ref[...] = (c * LOG2E).T[:N_FOX_HEADS, :]

    pfx = _dot(h, wfx_ref[...])
    seg = seg_ref[...]

    def head_norm(t, g):
        ms = _dot((t * t).astype(BF16), seg)
        return t * lax.rsqrt(ms + EPS) * g

    qfx_ref[...] = (head_norm(pfx[:, :FOX_WIDTH], gq_ref[...]) * scale).astype(BF16)
    kfx_ref[...] = head_norm(pfx[:, FOX_WIDTH:2 * FOX_WIDTH], gk_ref[...]).astype(BF16)
    vfx_ref[...] = pfx[:, 2 * FOX_WIDTH:].astype(BF16)

    psb = _dot(h, wsb_ref[...])
    qsb_ref[...] = (psb[:, :SB_WIDTH] * scale).astype(BF16)
    ksb_ref[...] = psb[:, SB_WIDTH:2 * SB_WIDTH].astype(BF16)
    vsb_ref[...] = psb[:, 2 * SB_WIDTH:].astype(BF16)


def _in_proj(x2d, g_mix, w_in_t, b_f, g_q, g_k, seg, *, seq):
    m = x2d.shape[0]
    tm = ROW_TILE
    qkv = 3 * SB_WIDTH
    assert OFF_FOX == qkv and OFF_FORGET == 2 * qkv and OFF_FORGET % LANES == 0
    const = lambda i: (0, 0)
    row = lambda i: (i, 0)
    full = lambda a: pl.BlockSpec(a.shape, const)
    once = pl.Buffered(1)
    out_w = lambda: pl.BlockSpec((tm, SB_WIDTH), row)
    return pl.pallas_call(
        functools.partial(_in_proj_kernel, tiles_per_seq=seq // tm),
        grid=(m // tm,),
        in_specs=[pl.BlockSpec((tm, D_MODEL), row), full(g_mix),
                  pl.BlockSpec((qkv, D_MODEL), lambda i: (0, 0), pipeline_mode=once),
                  pl.BlockSpec((qkv, D_MODEL), lambda i: (1, 0), pipeline_mode=once),
                  pl.BlockSpec((LANES, D_MODEL), lambda i: (OFF_FORGET // LANES, 0), pipeline_mode=once),
                  full(b_f), full(g_q), full(g_k), full(seg)],
        out_specs=[out_w(), out_w(), out_w(), out_w(), out_w(), out_w(),
                   pl.BlockSpec((N_FOX_HEADS, tm), lambda i: (0, i))],
        out_shape=[jax.ShapeDtypeStruct((m, SB_WIDTH), BF16)] * 6
        + [jax.ShapeDtypeStruct((N_FOX_HEADS, m), F32)],
        scratch_shapes=[pltpu.VMEM((8, LANES), F32), pltpu.VMEM((D_MODEL, qkv), BF16),
                        pltpu.VMEM((D_MODEL, qkv), BF16), pltpu.VMEM((D_MODEL, LANES), BF16)],
        compiler_params=pltpu.CompilerParams(dimension_semantics=("arbitrary",),
                                             vmem_limit_bytes=VMEM_LIMIT),
        name="in_proj",
    )(x2d, g_mix, w_in_t, w_in_t, w_in_t, b_f, g_q, g_k, seg)


def _head_lane_masks():
    lane = lax.broadcasted_iota(jnp.int32, (1, LANES), 1)
    return [(lane >= hh * HEAD_DIM) & (lane < (hh + 1) * HEAD_DIM) for hh in range(HEADS_PER_BLOCK)]


def _neg_abs(x):
    bits = lax.bitcast_convert_type(x, jnp.uint32) | jnp.uint32(0x80000000)
    return lax.bitcast_convert_type(bits, F32)


def _sb_weights(q, km, carry, later_sum, strict):
    z = _dot_nt(q, km)
    sp = jnp.maximum(z, 0.0) + jnp.log2(1.0 + jnp.exp2(_neg_abs(z)))
    if strict is not None:
        sp = jnp.where(strict, sp, 0.0)
    later = _dot(sp.astype(BF16), later_sum)
    w = jnp.exp2((z - sp) + later + jnp.concatenate([carry] * (z.shape[1] // LANES), axis=1))
    if strict is not None:
        w = jnp.where(strict, w, 0.0)
    total = later[:, :1] - sp[:, :1]
    return w.astype(BF16), carry + jnp.broadcast_to(total, carry.shape)


def _sb_kernel(q_ref, k_ref, v_ref, o_ref, acc_ref, carry_ref):
    tq, kc = Q_TILE, SB_KEY_CHUNK
    lanes = _head_lane_masks()
    later_sum = jnp.where(lax.broadcasted_iota(jnp.int32, (kc, kc), 0) > lax.broadcasted_iota(jnp.int32, (kc, kc), 1),
                          -1.0, 0.0).astype(BF16)
    acc_ref[...] = jnp.zeros_like(acc_ref)
    carry_ref[...] = jnp.zeros_like(carry_ref)

    def step(rows, k0, strict, valid=None):
        k_chunk = k_ref[pl.ds(k0, kc), :]
        v_chunk = v_ref[pl.ds(k0, kc), :]
        zero = jnp.zeros_like(k_chunk)
        ws, vms = [], []
        for hh in range(HEADS_PER_BLOCK):
            carry = carry_ref[hh, rows, :]
            if valid is not None:
                carry = jnp.where(valid, carry, NEG_BIG)
            w, carry = _sb_weights(q_ref[rows, :], jnp.where(lanes[hh], k_chunk, zero), carry, later_sum, strict)
            carry_ref[hh, rows, :] = carry
            ws.append(w)
            vms.append(jnp.where(lanes[hh], v_chunk, zero))
        acc_ref[rows, :] += _dot(jnp.concatenate(ws, axis=1), jnp.concatenate(vms, axis=0))

    def live(rows):
        return jnp.max(carry_ref[:, rows, :]) > SB_UNDERFLOW_LOG2

    tiles = [(t * tq, (pl.program_id(2) * SB_TILES_PER_STEP + t) * tq) for t in range(SB_TILES_PER_STEP)]

    for base, q0 in tiles:
        for jd in reversed(range(tq // kc)):
            r0 = jd * kc
            qpos = lax.broadcasted_iota(jnp.int32, (tq - r0, kc), 0)
            kpos = lax.broadcasted_iota(jnp.int32, (tq - r0, kc), 1)
            step(slice(base + r0, base + tq), pl.multiple_of(q0 + r0, kc), kpos < qpos)
        step(slice(base, base + kc), pl.multiple_of(jnp.maximum(q0 - kc, 0), kc), None, valid=q0 > 0)

    tile_live = [(q0 > 0) & live(slice(base, base + tq)) for base, q0 in tiles]

    @pl.when(functools.reduce(jnp.logical_or, tile_live))
    def _():
        for (base, q0), is_live in zip(tiles, tile_live):
            @pl.when(is_live)
            def _():
                for r0 in range(kc, tq, kc):
                    @pl.when(live(slice(base + r0, base + r0 + kc)))
                    def _():
                        step(slice(base + r0, base + r0 + kc), pl.multiple_of(q0 - kc, kc), None)

                def body(state):
                    it, _ = state
                    step(slice(base, base + tq), pl.multiple_of(q0 - (it + 1) * kc, kc), None)
                    return it + 1, live(slice(base, base + tq)).astype(jnp.int32)

                lax.while_loop(lambda s: (s[0] < q0 // kc) & (s[1] > 0), body,
                               (jnp.int32(1), live(slice(base, base + tq)).astype(jnp.int32)))
    o_ref[...] = acc_ref[...].astype(BF16)


def _sb_attention(q, k, v, *, batch, seq):
    m = q.shape[0]
    rows = SB_TILES_PER_STEP * Q_TILE
    nq = seq // rows
    n_blocks = SB_WIDTH // LANES
    q_spec = pl.BlockSpec((rows, LANES), lambda b, hp, qi: (b * nq + qi, hp))
    kv_spec = pl.BlockSpec((seq, LANES), lambda b, hp, qi: (b, hp))
    return pl.pallas_call(
        _sb_kernel,
        grid=(batch, n_blocks, nq),
        in_specs=[q_spec, kv_spec, kv_spec],
        out_specs=q_spec,
        out_shape=jax.ShapeDtypeStruct((m, SB_WIDTH), BF16),
        scratch_shapes=[pltpu.VMEM((rows, LANES), F32),
                        pltpu.VMEM((HEADS_PER_BLOCK, rows, LANES), F32)],
        compiler_params=pltpu.CompilerParams(dimension_semantics=("arbitrary",) * 3,
                                             vmem_limit_bytes=VMEM_LIMIT),
        name="sb_attn",
    )(q, k, v)


def _fox_kernel(cend_ref, thr_ref, q_ref, k_ref, v_ref, crow_ref, o_ref, z_ref, m_ref, acc_ref):
    tq, kc = Q_TILE, FOX_KEY_CHUNK
    b, hp = pl.program_id(0), pl.program_id(1)
    nq = pl.num_programs(2) * FOX_TILES_PER_STEP
    lane = lax.broadcasted_iota(jnp.int32, (1, LANES), 1)
    sub = lax.broadcasted_iota(jnp.int32, (N_FOX_HEADS, 1), 0)
    both = tuple(range(HEADS_PER_BLOCK))
    m_ref[...] = jnp.full_like(m_ref, NEG_BIG)

    def head_lanes(hh):
        return (lane >= hh * HEAD_DIM) & (lane < (hh + 1) * HEAD_DIM)

    def first_visible_chunk(qi, hh):
        head = hp * HEADS_PER_BLOCK + hh
        base = cend_ref[head, b * nq + jnp.maximum(qi - 1, 0)]

        def body(i, state):
            first, reach = state
            c = qi - 1 - i
            reach = reach & (cend_ref[head, b * nq + c] - base <= thr_ref[0, 0]).astype(jnp.int32)
            return jnp.where(reach > 0, c, first), reach

        return lax.fori_loop(0, qi, body, (qi, jnp.int32(1)))[0]

    def logits(t, k0, width, heads, causal):
        k_chunk = k_ref[pl.ds(k0, width), :]
        c_rows = crow_ref[:, pl.ds(k0, width)]
        zero = jnp.zeros_like(k_chunk)
        for hh in heads:
            ck = jnp.sum(jnp.where(sub == hp * HEADS_PER_BLOCK + hh, c_rows, 0.0), axis=0, keepdims=True)
            z = _dot_nt(q_ref[t * tq:(t + 1) * tq, :], jnp.where(head_lanes(hh), k_chunk, zero)) - ck
            if causal:
                qpos = lax.broadcasted_iota(jnp.int32, (tq, width), 0)
                kpos = lax.broadcasted_iota(jnp.int32, (tq, width), 1)
                z = jnp.where(kpos <= qpos, z, NEG_BIG)
            z_ref[t, hh, :, pl.ds(k0, width)] = z
            mx = m_ref[t, hh]
            for c in range(width // LANES):
                mx = jnp.maximum(mx, z[:, c * LANES:(c + 1) * LANES])
            m_ref[t, hh] = mx

    def weigh(t, k0, width, heads):
        v_chunk = v_ref[pl.ds(k0, width), :]
        one = jnp.ones_like(v_chunk)
        for hh in heads:
            row_max = m_ref[t, hh]
            p = [jnp.exp2(z_ref[t, hh, :, pl.ds(k0 + c * LANES, LANES)] - row_max) for c in range(width // LANES)]
            acc_ref[t, hh] += _dot(jnp.concatenate(p, axis=1).astype(BF16),
                                   jnp.where(head_lanes(hh), v_chunk, one))

    def walk(start, stop, widths, fn):
        n_wide = (stop - start) // widths[0]

        def body(j, _):
            fn(pl.multiple_of(start + j * widths[0], kc), widths[0])
            return 0

        lax.fori_loop(0, n_wide, body, 0)
        pos = start + n_wide * widths[0]
        for width in widths[1:]:
            take = stop - pos >= width

            @pl.when(take)
            def _():
                fn(pl.multiple_of(pos, kc), width)

            pos = pos + jnp.where(take, width, 0)

    plans = []
    for t in range(FOX_TILES_PER_STEP):
        qi = pl.program_id(2) * FOX_TILES_PER_STEP + t
        first = [first_visible_chunk(qi, hh) for hh in both]
        slow = jnp.where(first[0] <= first[1], 0, 1)
        plans.append((t, qi * tq, slow, jnp.minimum(first[0], first[1]) * kc, jnp.maximum(first[0], first[1]) * kc))

    for t, q0, slow, k_lo, k_hi in plans:
        walk(k_lo, k_hi, FOX_SOLO_WIDTHS, lambda k0, width: logits(t, k0, width, (slow,), False))
        walk(k_hi, q0, FOX_PAIR_WIDTHS, lambda k0, width: logits(t, k0, width, both, False))
    for t, q0, _, _, _ in plans:
        logits(t, pl.multiple_of(q0, kc), kc, both, True)
    for t, _, _, _, _ in plans:
        for hh in both:
            m_ref[t, hh] = jnp.broadcast_to(jnp.max(m_ref[t, hh], axis=1, keepdims=True), (tq, LANES))
    acc_ref[...] = jnp.zeros_like(acc_ref)
    for t, q0, slow, k_lo, k_hi in plans:
        walk(k_lo, k_hi, FOX_SOLO_WIDTHS, lambda k0, width: weigh(t, k0, width, (slow,)))
        walk(k_hi, q0 + tq, FOX_PAIR_WIDTHS, lambda k0, width: weigh(t, k0, width, both))

    for t, _, _, _, _ in plans:
        out = jnp.zeros((tq, LANES), F32)
        for hh in both:
            acc = acc_ref[t, hh]
            out = out + jnp.where(head_lanes(hh), acc / pltpu.roll(acc, HEAD_DIM, axis=1), 0.0)
        o_ref[t * tq:(t + 1) * tq, :] = out.astype(BF16)


def _fox_attention(q, k, v, c_row, qk_norm_bound, *, batch, seq):
    assert Q_TILE == FOX_KEY_CHUNK
    m = q.shape[0]
    tiles = FOX_TILES_PER_STEP
    nq = seq // (tiles * Q_TILE)
    n_blocks = FOX_WIDTH // LANES
    q_spec = pl.BlockSpec((tiles * Q_TILE, LANES), lambda b, hp, qi: (b * nq + qi, hp))
    kv_spec = pl.BlockSpec((seq, LANES), lambda b, hp, qi: (b, hp))
    c_end = c_row[:, FOX_KEY_CHUNK - 1::FOX_KEY_CHUNK]
    thr = (FOX_ZERO_LOG2 + 2.0 * qk_norm_bound).reshape(1, 1).astype(F32)
    smem = pl.BlockSpec(memory_space=pltpu.SMEM)
    return pl.pallas_call(
        _fox_kernel,
        grid=(batch, n_blocks, nq),
        in_specs=[smem, smem, q_spec, kv_spec, kv_spec,
                  pl.BlockSpec((N_FOX_HEADS, seq), lambda b, hp, qi: (0, b))],
        out_specs=q_spec,
        out_shape=jax.ShapeDtypeStruct((m, FOX_WIDTH), BF16),
        scratch_shapes=[pltpu.VMEM((tiles, HEADS_PER_BLOCK, Q_TILE, seq), F32),
                        pltpu.VMEM((tiles, HEADS_PER_BLOCK, Q_TILE, LANES), F32),
                        pltpu.VMEM((tiles, HEADS_PER_BLOCK, Q_TILE, LANES), F32)],
        compiler_params=pltpu.CompilerParams(dimension_semantics=("arbitrary",) * 3,
                                             vmem_limit_bytes=VMEM_LIMIT),
        name="fox_attn",
    )(c_end, thr, q, k, v, c_row)


def _out_proj_kernel(x_ref, g_ref, wga_ref, wgb_ref, wgc_ref, bg_ref, osb_ref, ofx_ref, wosb32_ref, wofx32_ref,
                     wout32_ref, o_ref, wg_ref, wosb_ref, wofx_ref, wout_ref):
    @pl.when(pl.program_id(0) == 0)
    def _():
        skip = OFF_GATE % D_MODEL
        gate_rows = jnp.concatenate([wga_ref[skip:, :], wgb_ref[...], wgc_ref[...]], axis=0)
        wg_ref[...] = gate_rows.T.astype(BF16)
        wosb_ref[...] = wosb32_ref[...].astype(BF16)
        wofx_ref[...] = wofx32_ref[...].astype(BF16)
        wout_ref[...] = wout32_ref[...].astype(BF16)

    x = x_ref[...]
    h = _rms_norm_rows(x, g_ref[...]).astype(BF16)
    gates = jax.nn.sigmoid(_dot(h, wg_ref[...]) + bg_ref[...])
    y_sb = _dot(osb_ref[...], wosb_ref[...])
    y_fx = _dot(ofx_ref[...], wofx_ref[...])
    mixed = gates[:, :D_MODEL] * y_sb + gates[:, D_MODEL:] * y_fx
    o_ref[...] = x + _dot(mixed.astype(BF16), wout_ref[...])


def _out_proj(x2d, g_mix, w_in_t, b_gate, o_sb, o_fx, w_o_sb, w_o_fox, w_out):
    m = x2d.shape[0]
    tm = ROW_TILE
    n_gate = N_BRANCHES * D_MODEL
    skip = OFF_GATE % D_MODEL
    assert skip % 8 == 0 and w_in_t.shape[0] == OFF_GATE + n_gate
    const = lambda i: (0, 0)
    row = lambda i: (i, 0)
    full = lambda a: pl.BlockSpec(a.shape, const)
    once = pl.Buffered(1)
    resident = lambda a: pl.BlockSpec(a.shape, const, pipeline_mode=once)
    blk_a = OFF_GATE // D_MODEL
    return pl.pallas_call(
        _out_proj_kernel,
        grid=(m // tm,),
        in_specs=[pl.BlockSpec((tm, D_MODEL), row), full(g_mix),
                  pl.BlockSpec((D_MODEL, D_MODEL), lambda i: (blk_a, 0), pipeline_mode=once),
                  pl.BlockSpec((D_MODEL, D_MODEL), lambda i: (blk_a + 1, 0), pipeline_mode=once),
                  pl.BlockSpec((skip, D_MODEL), lambda i: ((blk_a + 2) * D_MODEL // skip, 0), pipeline_mode=once),
                  full(b_gate),
                  pl.BlockSpec((tm, SB_WIDTH), row), pl.BlockSpec((tm, FOX_WIDTH), row),
                  resident(w_o_sb), resident(w_o_fox), resident(w_out)],
        out_specs=pl.BlockSpec((tm, D_MODEL), row),
        out_shape=jax.ShapeDtypeStruct((m, D_MODEL), F32),
        scratch_shapes=[pltpu.VMEM((D_MODEL, n_gate), BF16), pltpu.VMEM(w_o_sb.shape, BF16),
                        pltpu.VMEM(w_o_fox.shape, BF16), pltpu.VMEM(w_out.shape, BF16)],
        compiler_params=pltpu.CompilerParams(dimension_semantics=("arbitrary",),
                                             vmem_limit_bytes=VMEM_LIMIT),
        name="out_proj",
    )(x2d, g_mix, w_in_t, w_in_t, w_in_t, b_gate, o_sb, o_fx, w_o_sb, w_o_fox, w_out)


def _conv_ffn_kernel(x_ref, g_ref, wu_ref, cw_ref, cb_ref, wd_ref, o_ref, hext_ref, prev_ref, u0_ref, u1_ref,
                     act_ref, *, tiles_per_seq):
    tm, tf, halo = FFN_ROW_TILE, FFN_COL_TILE, BF16_ROWS
    nf = D_FF // tf
    h = _rms_norm_rows(x_ref[...], g_ref[...]).astype(BF16)
    first = pl.program_id(0) % tiles_per_seq == 0
    hext_ref[:halo, :] = jnp.where(first, jnp.zeros_like(prev_ref), prev_ref[...])
    hext_ref[halo:, :] = h
    prev_ref[...] = h[tm - halo:, :]
    o_ref[...] = x_ref[...]

    u_refs = (u0_ref, u1_ref)

    def cols(f, part):
        return pl.ds(pl.multiple_of(part * D_FF + f * tf, tf), tf)

    def up(f, slot):
        w = jnp.concatenate([wu_ref[:, cols(f, 0)], wu_ref[:, cols(f, 1)]], axis=1)
        u_refs[slot][...] = _dot(hext_ref[...], w)

    def act(f, slot, half):
        u = u_refs[slot]
        cw = jnp.concatenate([cw_ref[:, cols(f, 0)], cw_ref[:, cols(f, 1)]], axis=1)
        cb = jnp.concatenate([cb_ref[:, cols(f, 0)], cb_ref[:, cols(f, 1)]], axis=1)
        for r0 in range(0, tm, FFN_ACT_ROWS):
            y = cb + cw[2:3, :] * u[pl.ds(halo + r0, FFN_ACT_ROWS), :]
            y = y + cw[1:2, :] * u[pl.ds(halo + r0 - 1, FFN_ACT_ROWS), :]
            y = y + cw[0:1, :] * u[pl.ds(halo + r0 - 2, FFN_ACT_ROWS), :]
            gate, val = y[:, :tf], y[:, tf:]
            act_ref[r0:r0 + FFN_ACT_ROWS, half * tf:(half + 1) * tf] = (
                gate * jax.nn.sigmoid(gate) * val).astype(BF16)

    def down(f0, n_tiles):
        rows = pl.ds(pl.multiple_of(f0 * tf, tf), tf * n_tiles)
        o_ref[...] += _dot(act_ref[:, :tf * n_tiles], wd_ref[rows, :])

    assert nf % 2 == 1
    up(0, 0)

    def body(p, _):
        f = 2 * p
        up(f + 1, 1)
        act(f, 0, 0)
        up(f + 2, 0)
        act(f + 1, 1, 1)
        down(f, 2)
        return 0

    lax.fori_loop(0, nf // 2, body, 0)
    act(nf - 1, 0, 0)
    down(nf - 1, 1)


def _conv_ffn(x2d, g_ffn, w_up, conv_w, conv_b, w_down, *, seq):
    m = x2d.shape[0]
    tm, tf = FFN_ROW_TILE, FFN_COL_TILE
    row = lambda i: (i, 0)
    resident = lambda a: pl.BlockSpec(a.shape, lambda i: (0, 0), pipeline_mode=pl.Buffered(1))
    return pl.pallas_call(
        functools.partial(_conv_ffn_kernel, tiles_per_seq=seq // tm),
        grid=(m // tm,),
        in_specs=[pl.BlockSpec((tm, D_MODEL), row), resident(g_ffn), resident(w_up), resident(conv_w),
                  resident(conv_b), resident(w_down)],
        out_specs=pl.BlockSpec((tm, D_MODEL), row),
        out_shape=jax.ShapeDtypeStruct((m, D_MODEL), F32),
        scratch_shapes=[pltpu.VMEM((tm + BF16_ROWS, D_MODEL), BF16), pltpu.VMEM((BF16_ROWS, D_MODEL), BF16),
                        pltpu.VMEM((tm + BF16_ROWS, 2 * tf), F32), pltpu.VMEM((tm + BF16_ROWS, 2 * tf), F32),
                        pltpu.VMEM((tm, 2 * tf), BF16)],
        compiler_params=pltpu.CompilerParams(dimension_semantics=("arbitrary",),
                                             vmem_limit_bytes=VMEM_LIMIT),
        name="conv_ffn",
    )(x2d, g_ffn, w_up, conv_w, conv_b, w_down)


def kernel(x, g_mix, w_in, b_forget, b_gate, g_q, g_k, w_o_sb, w_o_fox, w_out,
           g_ffn, w_up, conv_w, conv_b, w_down):
    batch, seq, _ = x.shape
    depth = g_mix.shape[0]
    assert seq % Q_TILE == 0 and seq % FFN_ROW_TILE == 0 and seq % ROW_TILE == 0
    x2d = x.reshape(batch * seq, D_MODEL)
    head_of_lane = jnp.arange(FOX_WIDTH) // HEAD_DIM
    seg = jnp.where(head_of_lane[:, None] == head_of_lane[None, :], 1.0 / HEAD_DIM, 0.0).astype(BF16)
    row2d = lambda a: a.reshape(1, -1)
    for layer in range(depth):
        w_t = w_in[layer].T
        b_f = jnp.pad(b_forget[layer], (0, LANES - N_FOX_HEADS)).reshape(1, LANES)
        q_sb, k_sb, v_sb, q_fx, k_fx, v_fx, c_row = _in_proj(
            x2d, row2d(g_mix[layer]), w_t, b_f,
            row2d(jnp.tile(g_q[layer], N_FOX_HEADS)), row2d(jnp.tile(g_k[layer], N_FOX_HEADS)), seg, seq=seq)
        o_sb = _sb_attention(q_sb, k_sb, v_sb, batch=batch, seq=seq)
        qk_bound = (FOX_NORM_SLACK * HEAD_DIM ** 0.5 * LOG2E
                    * jnp.max(jnp.abs(g_q[layer])) * jnp.max(jnp.abs(g_k[layer])))
        o_fx = _fox_attention(q_fx, k_fx, v_fx, c_row, qk_bound, batch=batch, seq=seq)
        x2d = _out_proj(x2d, row2d(g_mix[layer]), w_t, row2d(b_gate[layer]), o_sb, o_fx,
                        w_o_sb[layer], w_o_fox[layer], w_out[layer])
        x2d = _conv_ffn(x2d, row2d(g_ffn[layer]), w_up[layer].astype(BF16), conv_w[layer],
                        row2d(conv_b[layer]), w_down[layer].astype(BF16), seq=seq)
    return x2d.reshape(batch, seq, D_MODEL)
```
